```python
import math
import jax, jax.numpy as jnp
from jax import lax
import numpy as np

D_MODEL = 2048
BATCH = 2
SEQ = 16384
DEPTH = 4

N_MIXERS = 4
WIDTH = D_MODEL
EPS = 1e-6
CONV_WIDTH = 31
HG_HEAD_DIM = 128
HG_HEADS = WIDTH // HG_HEAD_DIM
HG_CHUNK = 64
DA_HEAD_DIM = 128
DA_HEADS = WIDTH // (2 * DA_HEAD_DIM)
DA_V_DIM = 2 * DA_HEAD_DIM
ROPE_THETA = 500000.0
ROPE_DIM = DA_HEAD_DIM // 4
Q_BLOCK = 128
FN_GROUPS = 8
FN_GROUP_DIM = WIDTH // FN_GROUPS
N_CONV = len(range(0, DEPTH, N_MIXERS))
N_HGRN = len(range(1, DEPTH, N_MIXERS))
N_DIFF = len(range(2, DEPTH, N_MIXERS))
N_FNET = len(range(3, DEPTH, N_MIXERS))

kernel_name = "hybrid_conv_hgrn2_diffattn_fnet_encoder"


def rms_norm(x, g):
    xf = x.astype(jnp.float32)
    y = xf * lax.rsqrt(jnp.mean(xf * xf, axis=-1, keepdims=True) + EPS)
    return (y * g.astype(jnp.float32)).astype(x.dtype)


def layer_norm(x, g, b):
    xf = x.astype(jnp.float32)
    mu = jnp.mean(xf, axis=-1, keepdims=True)
    var = jnp.mean(jnp.square(xf - mu), axis=-1, keepdims=True)
    y = (xf - mu) * lax.rsqrt(var + EPS)
    return (y * g.astype(jnp.float32) + b.astype(jnp.float32)).astype(x.dtype)


def conformer_conv(h, w_in, dw, dw_b, ln_g, ln_b, w_out):
    u = h @ w_in
    a, b, z = jnp.split(u, 3, axis=-1)
    v = a * jax.nn.sigmoid(b)
    pad = CONV_WIDTH // 2
    v = lax.conv_general_dilated(
        v, dw[:, None, :].astype(v.dtype), window_strides=(1,), padding=[(pad, pad)],
        dimension_numbers=("NWC", "WIO", "NWC"), feature_group_count=WIDTH) + dw_b
    v = jax.nn.silu(layer_norm(v, ln_g, ln_b))
    return (v * jax.nn.silu(z)) @ w_out


def hgrn_lower_bounds(table):
    lb = jnp.cumsum(jax.nn.softmax(table.astype(jnp.float32), axis=0), axis=0)
    return lb - lb[0:1]


def hgrn_direction(q, k, v, log_f):
    B, H, S, _ = q.shape
    nc = S // HG_CHUNK

    def chunks(t):
        return jnp.moveaxis(t.reshape(B, H, nc, HG_CHUNK, t.shape[-1]), 2, 0)

    lower_tri = jnp.tril(jnp.ones((HG_CHUNK, HG_CHUNK), dtype=bool))[:, :, None]

    def step(state, inp):
        qb, kb, vb, fb = inp
        b = jnp.cumsum(fb, axis=2)
        o_inter = jnp.einsum("bhtk,bhkv->bhtv", qb * jnp.exp(b), state)
        diff = b[:, :, :, None, :] - b[:, :, None, :, :]
        decay = jnp.exp(jnp.where(lower_tri, diff, -jnp.inf))
        scores = jnp.einsum("bhtk,bhsk,bhtsk->bhts", qb, kb, decay)
        o_intra = jnp.einsum("bhts,bhsv->bhtv", scores, vb)
        b_last = b[:, :, -1:, :]
        new_state = state * jnp.exp(b_last)[:, :, 0, :, None] + jnp.einsum(
            "bhsk,bhsv->bhkv", kb * jnp.exp(b_last - b), vb)
        return new_state, o_inter + o_intra

    state0 = jnp.zeros((B, H, q.shape[-1], v.shape[-1]), jnp.float32)
    _, out = lax.scan(step, state0, (chunks(q), chunks(k), chunks(v), chunks(log_f)))
    return jnp.moveaxis(out, 0, 2).reshape(B, H, S, v.shape[-1])


def hgrn2_mixer(h, w_in, lb_fwd, lb_bwd, o_norm, w_out):
    B, S, _ = h.shape
    u = h @ w_in
    q, a_f, a_b, i, z = jnp.split(u, 5, axis=-1)

    def heads(t):
        return t.astype(jnp.float32).reshape(B, S, HG_HEADS, HG_HEAD_DIM).transpose(0, 2, 1, 3)

    def forget(a, lb):
        a = a.astype(jnp.float32)
        log_f = jnp.logaddexp(jnp.log(lb), jnp.log1p(-lb) + jax.nn.log_sigmoid(a))
        key = (1.0 - lb) * jax.nn.sigmoid(-a)
        return heads(log_f), heads(key)

    qh, vh = heads(q), heads(i)
    lf_f, k_f = forget(a_f, lb_fwd)
    lf_b, k_b = forget(a_b, lb_bwd)
    o_fwd = hgrn_direction(qh, k_f, vh, lf_f)
    flip = lambda t: jnp.flip(t, axis=2)
    o_bwd = flip(hgrn_direction(flip(qh), flip(k_b), flip(vh), flip(lf_b)))
    o = rms_norm(o_fwd + o_bwd, o_norm)
    o = o.transpose(0, 2, 1, 3).reshape(B, S, WIDTH).astype(h.dtype)
    return (o * jax.nn.silu(z)) @ w_out


def rope_tables(positions):
    inv = 1.0 / (ROPE_THETA ** (jnp.arange(0, ROPE_DIM, 2, dtype=jnp.float32) / ROPE_DIM))
    ang = positions.astype(jnp.float32)[..., None] * inv
    return jnp.cos(ang), jnp.sin(ang)


def apply_partial_rope(x, cos, sin):
    xr, xp = x[..., :ROPE_DIM], x[..., ROPE_DIM:]
    x1, x2 = jnp.split(xr.astype(jnp.float32), 2, axis=-1)
    c = cos[:, :, None, None, :]
    s = sin[:, :, None, None, :]
    rot = jnp.concatenate([x1 * c - x2 * s, x2 * c + x1 * s], axis=-1)
    return jnp.concatenate([rot.astype(x.dtype), xp], axis=-1)


def diff_attention(h, cos, sin, w_in, q_norm, k_norm, lam_q1, lam_k1, lam_q2, lam_k2,
                   sub_norm, w_out, lam_init):
    B, S, _ = h.shape
    u = h @ w_in
    q, k, v, z = jnp.split(u, 4, axis=-1)
    q = rms_norm(q.reshape(B, S, DA_HEADS, 2, DA_HEAD_DIM), q_norm)
    k = rms_norm(k.reshape(B, S, DA_HEADS, 2, DA_HEAD_DIM), k_norm)
    q = apply_partial_rope(q, cos, sin).transpose(0, 2, 3, 1, 4)
    k = apply_partial_rope(k, cos, sin).transpose(0, 2, 3, 1, 4)
    v = v.reshape(B, S, DA_HEADS, DA_V_DIM).transpose(0, 2, 1, 3)
    f32 = jnp.float32
    lam = (jnp.exp(jnp.sum(lam_q1.astype(f32) * lam_k1.astype(f32)))
           - jnp.exp(jnp.sum(lam_q2.astype(f32) * lam_k2.astype(f32))) + lam_init)
    scale = 1.0 / math.sqrt(DA_HEAD_DIM)
    nq = S // Q_BLOCK
    qb_all = q.reshape(B, DA_HEADS, 2, nq, Q_BLOCK, DA_HEAD_DIM).transpose(3, 0, 1, 2, 4, 5)

    def block(qb):
        s = jnp.einsum("bhmqd,bhmkd->bhmqk", qb, k).astype(f32) * scale
        p = jax.nn.softmax(s, axis=-1)
        w = p[:, :, 0] - lam * p[:, :, 1]
        return jnp.einsum("bhqk,bhkv->bhqv", w.astype(v.dtype), v)

    o = lax.map(block, qb_all)
    o = o.transpose(1, 2, 0, 3, 4).reshape(B, DA_HEADS, S, DA_V_DIM)
    o = rms_norm(o, sub_norm) * (1.0 - lam_init)
    o = o.transpose(0, 2, 1, 3).reshape(B, S, WIDTH).astype(h.dtype)
    return (o * jax.nn.silu(z)) @ w_out


def fourier_mixer(h, w_in, group_w, w_out):
    B, S, _ = h.shape
    u = h @ w_in
    v, z = jnp.split(u, 2, axis=-1)
    vg = v.astype(jnp.float32).reshape(B, S, FN_GROUPS, FN_GROUP_DIM)
    f = jnp.fft.fftn(vg, axes=(1, 3), norm="ortho").real
    y = jnp.einsum("bsgc,gce->bsge", f.astype(h.dtype), group_w).reshape(B, S, WIDTH)
    return (y * jax.nn.silu(z)) @ w_out


def setup_inputs(seed: int = 0) -> dict:
    key = jax.random.key(seed)
    ks = iter(jax.random.split(key, 40))
    E, D = WIDTH, D_MODEL
    nrm = lambda shape, s: jax.random.normal(next(ks), shape, jnp.float32) * s
    gain = lambda shape: 1.0 + nrm(shape, 0.02)
    return {
        "x": nrm((BATCH, SEQ, D), 1.0),
        "positions": jnp.broadcast_to(jnp.arange(SEQ, dtype=jnp.int32)[None, :], (BATCH, SEQ)),
        "conv_norm": gain((N_CONV, D)),
        "conv_w_in": nrm((N_CONV, D, 3 * E), D ** -0.5),
        "conv_dw": nrm((N_CONV, CONV_WIDTH, E), CONV_WIDTH ** -0.5),
        "conv_dw_b": nrm((N_CONV, E), 0.02),
        "conv_ln_g": gain((N_CONV, E)),
        "conv_ln_b": nrm((N_CONV, E), 0.02),
        "conv_w_out": nrm((N_CONV, E, D), E ** -0.5),
        "hgrn_norm": gain((N_HGRN, D)),
        "hgrn_w_in": nrm((N_HGRN, D, 5 * E), D ** -0.5),
        "hgrn_lb_fwd": nrm((DEPTH, E), 0.1),
        "hgrn_lb_bwd": nrm((DEPTH, E), 0.1),
        "hgrn_o_norm": gain((N_HGRN, HG_HEAD_DIM)),
        "hgrn_w_out": nrm((N_HGRN, E, D), E ** -0.5),
        "diff_norm": gain((N_DIFF, D)),
        "diff_w_in": nrm((N_DIFF, D, 4 * E), D ** -0.5),
        "diff_q_norm": gain((N_DIFF, DA_HEAD_DIM)),
        "diff_k_norm": gain((N_DIFF, DA_HEAD_DIM)),
        "diff_lam_q1": nrm((N_DIFF, DA_HEAD_DIM), 0.1),
        "diff_lam_k1": nrm((N_DIFF, DA_HEAD_DIM), 0.1),
        "diff_lam_q2": nrm((N_DIFF, DA_HEAD_DIM), 0.1),
        "diff_lam_k2": nrm((N_DIFF, DA_HEAD_DIM), 0.1),
        "diff_sub_norm": gain((N_DIFF, DA_V_DIM)),
        "diff_w_out": nrm((N_DIFF, E, D), E ** -0.5),
        "fnet_norm": gain((N_FNET, D)),
        "fnet_w_in": nrm((N_FNET, D, 2 * E), D ** -0.5),
        "fnet_group_w": nrm((N_FNET, FN_GROUPS, FN_GROUP_DIM, FN_GROUP_DIM), FN_GROUP_DIM ** -0.5),
        "fnet_w_out": nrm((N_FNET, E, D), E ** -0.5),
    }


def reference(x, positions, conv_norm, conv_w_in, conv_dw, conv_dw_b, conv_ln_g, conv_ln_b, conv_w_out,
              hgrn_norm, hgrn_w_in, hgrn_lb_fwd, hgrn_lb_bwd, hgrn_o_norm, hgrn_w_out,
              diff_norm, diff_w_in, diff_q_norm, diff_k_norm, diff_lam_q1, diff_lam_k1, diff_lam_q2,
              diff_lam_k2, diff_sub_norm, diff_w_out,
              fnet_norm, fnet_w_in, fnet_group_w, fnet_w_out):
    cos, sin = rope_tables(positions)
    lb_fwd = hgrn_lower_bounds(hgrn_lb_fwd)
    lb_bwd = hgrn_lower_bounds(hgrn_lb_bwd)
    for layer in range(DEPTH):
        m, j = layer % N_MIXERS, layer // N_MIXERS
        if m == 0:
            h = rms_norm(x, conv_norm[j])
            y = conformer_conv(h, conv_w_in[j], conv_dw[j], conv_dw_b[j], conv_ln_g[j], conv_ln_b[j],
                               conv_w_out[j])
        elif m == 1:
            h = rms_norm(x, hgrn_norm[j])
            y = hgrn2_mixer(h, hgrn_w_in[j], lb_fwd[layer], lb_bwd[layer], hgrn_o_norm[j], hgrn_w_out[j])
        elif m == 2:
            h = rms_norm(x, diff_norm[j])
            lam_init = 0.8 - 0.6 * math.exp(-0.3 * layer)
            y = diff_attention(h, cos, sin, diff_w_in[j], diff_q_norm[j], diff_k_norm[j], diff_lam_q1[j],
                               diff_lam_k1[j], diff_lam_q2[j], diff_lam_k2[j], diff_sub_norm[j],
                               diff_w_out[j], lam_init)
        else:
            h = rms_norm(x, fnet_norm[j])
            y = fourier_mixer(h, fnet_w_in[j], fnet_group_w[j], fnet_w_out[j])
        x = x + y.astype(x.dtype)
    return x
```

```python
import functools
import math

import numpy as np
import jax
import jax.numpy as jnp
from jax import lax
from jax.experimental import pallas as pl
from jax.experimental.pallas import tpu as pltpu

EPS = 1e-6
LANES = 128
HEAD_DIM = 128
CONV_HALO = 16
ROPE_DIM = 32
ROPE_THETA = 500000.0
FN_GROUPS = 8
MIB = 1 << 20

F32 = jnp.float32
BF16 = jnp.bfloat16
NT_DIMS = (((1,), (1,)), ((), ()))


def _params(semantics, vmem_mib):
    return pltpu.CompilerParams(dimension_semantics=semantics, vmem_limit_bytes=vmem_mib * MIB)


def _silu(z):
    return z * jax.nn.sigmoid(z)


def _norm_matmul_kernel(x_ref, g_ref, w_ref, o_ref, h_ref):
    @pl.when(pl.program_id(1) == 0)
    def _():
        x = x_ref[...]
        ms = jnp.mean(x * x, axis=-1, keepdims=True)
        h_ref[...] = (x * lax.rsqrt(ms + EPS) * g_ref[...]).astype(h_ref.dtype)

    o_ref[...] = jnp.dot(h_ref[...], w_ref[...], preferred_element_type=F32).astype(o_ref.dtype)


def norm_matmul(x2d, gain, w, *, tm=1024, tn=1024):
    T, D = x2d.shape
    N = w.shape[1]
    tm, tn = min(tm, T), min(tn, N)
    return pl.pallas_call(
        _norm_matmul_kernel,
        grid=(T // tm, N // tn),
        in_specs=[
            pl.BlockSpec((tm, D), lambda i, j: (i, 0)),
            pl.BlockSpec((1, D), lambda i, j: (0, 0)),
            pl.BlockSpec((D, tn), lambda i, j: (0, j)),
        ],
        out_specs=pl.BlockSpec((tm, tn), lambda i, j: (i, j)),
        out_shape=jax.ShapeDtypeStruct((T, N), F32),
        scratch_shapes=[pltpu.VMEM((tm, D), BF16)],
        compiler_params=_params(("parallel", "arbitrary"), 48),
        name="norm_matmul",
    )(x2d, gain.reshape(1, D), w.astype(BF16))


def _gate_out_kernel(o_ref, z_ref, x_ref, w_ref, out_ref):
    g = (o_ref[...] * _silu(z_ref[...])).astype(BF16)
    out_ref[...] = x_ref[...] + jnp.dot(g, w_ref[...], preferred_element_type=F32)


def gate_out(o2d, u2d, z_block, x2d, w_out, *, tm=256):
    T, E = o2d.shape
    D = w_out.shape[1]
    tm = min(tm, T)
    return pl.pallas_call(
        _gate_out_kernel,
        grid=(T // tm,),
        in_specs=[
            pl.BlockSpec((tm, E), lambda i: (i, 0)),
            pl.BlockSpec((tm, E), lambda i: (i, z_block)),
            pl.BlockSpec((tm, D), lambda i: (i, 0)),
            pl.BlockSpec((E, D), lambda i: (0, 0)),
        ],
        out_specs=pl.BlockSpec((tm, D), lambda i: (i, 0)),
        out_shape=jax.ShapeDtypeStruct((T, D), F32),
        compiler_params=_params(("parallel",), 48),
        name="gate_out",
    )(o2d, u2d, x2d, w_out.astype(BF16))


def _conv_kernel(am_ref, ap_ref, an_ref, bm_ref, bp_ref, bn_ref, dw_ref, dwb_ref, g_ref, beta_ref,
                 o_ref, vbuf, *, taps, row_block, col_block):
    s = pl.program_id(1)
    ts, E = am_ref.shape[1], am_ref.shape[2]
    glu = lambda a, b: a * jax.nn.sigmoid(b)
    vbuf[0:CONV_HALO, :] = jnp.where(s > 0, glu(ap_ref[0], bp_ref[0]), 0.0)
    vbuf[CONV_HALO:CONV_HALO + ts, :] = glu(am_ref[0], bm_ref[0])
    vbuf[CONV_HALO + ts:, :] = jnp.where(s < pl.num_programs(1) - 1, glu(an_ref[0], bn_ref[0]), 0.0)
    first = CONV_HALO - taps // 2
    for r0 in range(0, ts, row_block):
        for c0 in range(0, E, col_block):
            cols = slice(c0, c0 + col_block)
            acc = jnp.zeros((row_block, col_block), F32)
            for k in range(taps):
                acc = acc + dw_ref[k:k + 1, cols] * vbuf[r0 + first + k:r0 + first + k + row_block, cols]
            o_ref[0, r0:r0 + row_block, cols] = acc + dwb_ref[:, cols]
    c = o_ref[0]
    mu = jnp.mean(c, axis=-1, keepdims=True)
    cc = c - mu
    var = jnp.mean(cc * cc, axis=-1, keepdims=True)
    y = cc * lax.rsqrt(var + EPS) * g_ref[...] + beta_ref[...]
    o_ref[0] = _silu(y)


def conv_mixer(u, dw, dw_b, ln_g, ln_b, *, ts=256):
    B, S, N = u.shape
    E = N // 3
    taps = dw.shape[0]
    ts = min(ts, S)
    hb = ts // CONV_HALO
    last_halo = S // CONV_HALO - 1
    main = lambda col: pl.BlockSpec((1, ts, E), lambda b, s: (b, s, col))
    prev = lambda col: pl.BlockSpec((1, CONV_HALO, E), lambda b, s: (b, jnp.maximum(s * hb - 1, 0), col))
    nxt = lambda col: pl.BlockSpec((1, CONV_HALO, E), lambda b, s: (b, jnp.minimum((s + 1) * hb, last_halo), col))
    vec = lambda rows: pl.BlockSpec((rows, E), lambda b, s: (0, 0))
    return pl.pallas_call(
        functools.partial(_conv_kernel, taps=taps, row_block=32, col_block=512),
        grid=(B, S // ts),
        in_specs=[main(0), prev(0), nxt(0), main(1), prev(1), nxt(1), vec(taps), vec(1), vec(1), vec(1)],
        out_specs=pl.BlockSpec((1, ts, E), lambda b, s: (b, s, 0)),
        out_shape=jax.ShapeDtypeStruct((B, S, E), F32),
        scratch_shapes=[pltpu.VMEM((ts + 2 * CONV_HALO, E), F32)],
        compiler_params=_params(("parallel", "parallel"), 40),
        name="conv_mixer",
    )(u, u, u, u, u, u, dw, dw_b.reshape(1, E), ln_g.reshape(1, E), ln_b.reshape(1, E))


def _hgrn_kernel(*refs, reverse, finalize):
    if finalize:
        q_ref, a_ref, v_ref, lb_ref, prev_ref, gn_ref, o_ref, st_ref = refs
    else:
        q_ref, a_ref, v_ref, lb_ref, o_ref, st_ref = refs

    @pl.when(pl.program_id(2) == 0)
    def _():
        st_ref[...] = jnp.zeros_like(st_ref)

    q, a, v, lb = q_ref[0], a_ref[0], v_ref[0], lb_ref[...]
    C = q.shape[0]
    e = jnp.exp(-jnp.abs(a))
    r = 1.0 / (1.0 + e)
    er = e * r
    pos = a >= 0
    lf = jnp.log(lb + (1.0 - lb) * jnp.where(pos, r, er))
    kk = (1.0 - lb) * jnp.where(pos, er, r)

    row = lax.broadcasted_iota(jnp.int32, (C, HEAD_DIM), 0)
    shift_back = (lambda x, d: pltpu.roll(x, d, 0))
    shift_fwd = (lambda x, d: pltpu.roll(x, C - d, 0))
    b = lf
    d = 1
    while d < C:
        if reverse:
            b = b + jnp.where(row < C - d, shift_fwd(b, d), 0.0)
        else:
            b = b + jnp.where(row >= d, shift_back(b, d), 0.0)
        d *= 2
    b_excl = b - lf
    total = b[0:1] if reverse else b[C - 1:C]

    blk_lo = b if reverse else b_excl
    blk_hi = b_excl if reverse else b
    tt = lax.broadcasted_iota(jnp.int32, (C, C), 0)
    ss = lax.broadcasted_iota(jnp.int32, (C, C), 1)
    xor = tt ^ ss
    scores = jnp.zeros((C, C), F32)
    m, level = 1, 0
    while m < C:
        bit = (row & m) != 0
        q_ref_pt, k_ref_pt = (blk_hi, blk_lo) if reverse else (blk_lo, blk_hi)
        q_active = jnp.logical_not(bit) if reverse else bit
        qm = jnp.where(q_active, q * jnp.exp(b - q_ref_pt), 0.0).astype(BF16)
        km = (kk * jnp.exp(k_ref_pt - b)).astype(BF16)
        sc = lax.dot_general(qm, km, NT_DIMS, preferred_element_type=F32)
        scores = scores + jnp.where((xor >> level) == 1, sc, 0.0)
        blk_lo = jnp.where(bit, shift_back(blk_lo, m), blk_lo)
        blk_hi = jnp.where(bit, blk_hi, shift_fwd(blk_hi, m))
        m *= 2
        level += 1

    vb = v.astype(BF16)
    st = st_ref[...]
    o = jnp.dot(scores.astype(BF16), vb, preferred_element_type=F32)
    o = o + jnp.sum(q * kk, axis=1, keepdims=True) * v
    o = o + lax.dot_general((q * jnp.exp(b)).astype(BF16), st.astype(BF16), NT_DIMS, preferred_element_type=F32)
    kd = (kk * jnp.exp(total - b)).astype(BF16)
    st_ref[...] = st * jnp.exp(total) + jnp.dot(vb.T, kd, preferred_element_type=F32)

    if finalize:
        o = o + prev_ref[0]
        ms = jnp.mean(o * o, axis=-1, keepdims=True)
        o = o * lax.rsqrt(ms + EPS) * gn_ref[...]
    o_ref[0] = o


def hgrn_direction(u, lb, gate_block, *, reverse, prev=None, o_norm=None, chunk=256):
    B, S, N = u.shape
    E = N // 5
    H = E // HEAD_DIM
    C = min(chunk, S)
    nc = S // C
    finalize = prev is not None
    cidx = (lambda c: nc - 1 - c) if reverse else (lambda c: c)
    col = lambda blk: pl.BlockSpec((1, C, HEAD_DIM), lambda b, h, c: (b, cidx(c), blk * H + h))
    in_specs = [col(0), col(gate_block), col(3), pl.BlockSpec((1, HEAD_DIM), lambda b, h, c: (0, h))]
    args = [u, u, u, lb.reshape(1, E)]
    if finalize:
        in_specs += [pl.BlockSpec((1, C, HEAD_DIM), lambda b, h, c: (b, cidx(c), h)),
                     pl.BlockSpec((1, HEAD_DIM), lambda b, h, c: (0, 0))]
        args += [prev, o_norm.reshape(1, HEAD_DIM)]
    return pl.pallas_call(
        functools.partial(_hgrn_kernel, reverse=reverse, finalize=finalize),
        grid=(B, H, nc),
        in_specs=in_specs,
        out_specs=pl.BlockSpec((1, C, HEAD_DIM), lambda b, h, c: (b, cidx(c), h)),
        out_shape=jax.ShapeDtypeStruct((B, S, E), F32),
        scratch_shapes=[pltpu.VMEM((HEAD_DIM, HEAD_DIM), F32)],
        compiler_params=_params(("parallel", "parallel", "arbitrary"), 32),
        name="hgrn_bwd" if reverse else "hgrn_fwd",
    )(*args)


def _attn_prep_kernel(q_ref, k_ref, v_ref, cf_ref, sa_ref, sb_ref, qn_ref, kn_ref, qo_ref, ko_ref, vo_ref,
                      *, q_scale):
    cf, sa, sb = cf_ref[...], sa_ref[...], sb_ref[...]
    half = ROPE_DIM // 2

    def norm_rope(x, gain):
        ms = jnp.mean(x * x, axis=-1, keepdims=True)
        xn = x * lax.rsqrt(ms + EPS) * gain
        return xn * cf + pltpu.roll(xn, LANES - half, 1) * sa + pltpu.roll(xn, half, 1) * sb

    for g in range(q_ref.shape[1] // HEAD_DIM):
        sl = slice(g * HEAD_DIM, (g + 1) * HEAD_DIM)
        qo_ref[:, sl] = (norm_rope(q_ref[:, sl], qn_ref[...]) * q_scale).astype(BF16)
        ko_ref[:, sl] = norm_rope(k_ref[:, sl], kn_ref[...]).astype(BF16)
    vo_ref[...] = v_ref[...].astype(BF16)


def attn_prep(u2d, rope_cf, rope_sa, rope_sb, q_norm, k_norm, *, tm=512):
    T, N = u2d.shape
    E = N // 4
    tm = min(tm, T)
    col = lambda blk: pl.BlockSpec((tm, E), lambda i: (i, blk))
    tab = pl.BlockSpec((tm, LANES), lambda i: (i, 0))
    gain = pl.BlockSpec((1, HEAD_DIM), lambda i: (0, 0))
    out = pl.BlockSpec((tm, E), lambda i: (i, 0))
    return pl.pallas_call(
        functools.partial(_attn_prep_kernel, q_scale=1.0 / math.sqrt(HEAD_DIM)),
        grid=(T // tm,),
        in_specs=[col(0), col(1), col(2), tab, tab, tab, gain, gain],
        out_specs=[out, out, out],
        out_shape=[jax.ShapeDtypeStruct((T, E), BF16)] * 3,
        compiler_params=_params(("parallel",), 48),
        name="attn_prep",
    )(u2d, u2d, u2d, rope_cf, rope_sa, rope_sb, q_norm.reshape(1, HEAD_DIM), k_norm.reshape(1, HEAD_DIM))


def _attn_kernel(lam_ref, q_ref, k_ref, v_ref, sn_ref, o_ref, m_ref, l_ref, acc_ref, *, post_scale):
    kv = pl.program_id(3)

    @pl.when(kv == 0)
    def _():
        m_ref[...] = jnp.full_like(m_ref, -jnp.inf)
        l_ref[...] = jnp.zeros_like(l_ref)
        acc_ref[...] = jnp.zeros_like(acc_ref)

    v = v_ref[0]
    for i in range(2):
        sl = slice(i * HEAD_DIM, (i + 1) * HEAD_DIM)
        s = lax.dot_general(q_ref[0, :, sl], k_ref[0, :, sl], NT_DIMS, preferred_element_type=F32)
        m_prev = m_ref[i]
        m_new = jnp.maximum(m_prev, jnp.max(s, axis=-1, keepdims=True))
        alpha = jnp.exp(m_prev - m_new)
        p = jnp.exp(s - m_new)
        l_ref[i] = alpha * l_ref[i] + jnp.sum(p, axis=-1, keepdims=True)
        acc_ref[i] = alpha * acc_ref[i] + jnp.dot(p.astype(BF16), v, preferred_element_type=F32)
        m_ref[i] = m_new

    @pl.when(kv == pl.num_programs(3) - 1)
    def _():
        o = acc_ref[0] / l_ref[0] - lam_ref[0] * (acc_ref[1] / l_ref[1])
        ms = jnp.mean(o * o, axis=-1, keepdims=True)
        o_ref[0] = o * lax.rsqrt(ms + EPS) * sn_ref[...] * post_scale


def diff_attention_core(qp, kp, vp, lam, sub_norm, post_scale, *, tq=512, tk=512):
    B, S, E = qp.shape
    W = 2 * HEAD_DIM
    H = E // W
    tq, tk = min(tq, S), min(tk, S)
    return pl.pallas_call(
        functools.partial(_attn_kernel, post_scale=post_scale),
        grid=(B, H, S // tq, S // tk),
        in_specs=[
            pl.BlockSpec(memory_space=pltpu.SMEM),
            pl.BlockSpec((1, tq, W), lambda b, h, i, j: (b, i, h)),
            pl.BlockSpec((1, tk, W), lambda b, h, i, j: (b, j, h)),
            pl.BlockSpec((1, tk, W), lambda b, h, i, j: (b, j, h)),
            pl.BlockSpec((1, W), lambda b, h, i, j: (0, 0)),
        ],
        out_specs=pl.BlockSpec((1, tq, W), lambda b, h, i, j: (b, i, h)),
        out_shape=jax.ShapeDtypeStruct((B, S, E), F32),
        scratch_shapes=[pltpu.VMEM((2, tq, 1), F32), pltpu.VMEM((2, tq, 1), F32), pltpu.VMEM((2, tq, W), F32)],
        compiler_params=_params(("parallel", "parallel", "parallel", "arbitrary"), 32),
        name="diff_attention",
    )(lam.reshape(1), qp, kp, vp, sub_norm.reshape(1, W))


def _dft_tables(n):
    ang = 2.0 * np.pi * ((np.arange(n)[:, None] * np.arange(n)[None, :]) % n) / n
    return jnp.asarray(np.cos(ang), F32), jnp.asarray(np.sin(ang), F32)


def _fnet_a_kernel(x_ref, c_ref, s_ref, tc_ref, ts_ref, ar_ref, ai_ref):
    x = x_ref[0].astype(BF16)
    ar = jnp.dot(c_ref[...], x, preferred_element_type=F32)
    ai = -jnp.dot(s_ref[...], x, preferred_element_type=F32)
    tc, ts = tc_ref[0], ts_ref[0]
    ar_ref[0] = ar * tc + ai * ts
    ai_ref[0] = ai * tc - ar * ts


def _fnet_b_kernel(ar_ref, ai_ref, c_ref, s_ref, cc_ref, sc_ref, gw_ref, y_ref, *, norm):
    ar, ai = ar_ref[0].astype(BF16), ai_ref[0].astype(BF16)
    c, s = c_ref[...], s_ref[...]
    br = jnp.dot(c, ar, preferred_element_type=F32) + jnp.dot(s, ai, preferred_element_type=F32)
    bi = jnp.dot(c, ai, preferred_element_type=F32) - jnp.dot(s, ar, preferred_element_type=F32)
    G = gw_ref.shape[1]
    for g in range(gw_ref.shape[0]):
        sl = slice(g * G, (g + 1) * G)
        f = (jnp.dot(br[:, sl].astype(BF16), cc_ref[...], preferred_element_type=F32)
             + jnp.dot(bi[:, sl].astype(BF16), sc_ref[...], preferred_element_type=F32)) * norm
        y_ref[0, :, sl] = jnp.dot(f.astype(BF16), gw_ref[g], preferred_element_type=F32)


def fourier_core(u, group_w, *, n_lo=128):
    B, S, N = u.shape
    E = N // 2
    G = E // FN_GROUPS
    n_hi = S // n_lo
    c_hi, s_hi = _dft_tables(n_hi)
    c_lo, s_lo = _dft_tables(n_lo)
    c_ch, s_ch = _dft_tables(G)
    ang = 2.0 * np.pi * ((np.arange(n_lo)[:, None] * np.arange(n_hi)[None, :]) % S) / S
    tw_c = jnp.asarray(np.cos(ang), F32)[:, :, None]
    tw_s = jnp.asarray(np.sin(ang), F32)[:, :, None]
    mat = lambda n: pl.BlockSpec((n, n), lambda b, j: (0, 0))
    tw = pl.BlockSpec((1, n_hi, 1), lambda b, j: (j, 0, 0))
    a_out = pl.BlockSpec((1, n_hi, E), lambda b, j: (b, 0, j))
    ar, ai = pl.pallas_call(
        _fnet_a_kernel,
        grid=(B, n_lo),
        in_specs=[pl.BlockSpec((1, n_hi, E), lambda b, j: (b, 0, 2 * j)), mat(n_hi), mat(n_hi), tw, tw],
        out_specs=[a_out, a_out],
        out_shape=[jax.ShapeDtypeStruct((B, n_hi, n_lo * E), F32)] * 2,
        compiler_params=_params(("parallel", "parallel"), 32),
        name="fnet_stage_a",
    )(u.reshape(B, n_hi, n_lo * N), c_hi.astype(BF16), s_hi.astype(BF16), tw_c, tw_s)

    a_in = pl.BlockSpec((1, n_lo, E), lambda b, j: (b * n_hi + j, 0, 0))
    y = pl.pallas_call(
        functools.partial(_fnet_b_kernel, norm=1.0 / math.sqrt(S * G)),
        grid=(B, n_hi),
        in_specs=[a_in, a_in, mat(n_lo), mat(n_lo), mat(G), mat(G),
                  pl.BlockSpec((FN_GROUPS, G, G), lambda b, j: (0, 0, 0))],
        out_specs=pl.BlockSpec((1, n_lo, E), lambda b, j: (b, 0, j)),
        out_shape=jax.ShapeDtypeStruct((B, n_lo, n_hi * E), F32),
        compiler_params=_params(("parallel", "parallel"), 32),
        name="fnet_stage_b",
    )(ar.reshape(B * n_hi, n_lo, E), ai.reshape(B * n_hi, n_lo, E), c_lo.astype(BF16), s_lo.astype(BF16),
      c_ch.astype(BF16), s_ch.astype(BF16), group_w.astype(BF16))
    return y.reshape(B, S, E)


def conv_layer(x, norm, w_in, dw, dw_b, ln_g, ln_b, w_out):
    B, S, D = x.shape
    E = w_out.shape[0]
    u = norm_matmul(x.reshape(B * S, D), norm, w_in)
    o = conv_mixer(u.reshape(B, S, 3 * E), dw, dw_b, ln_g, ln_b)
    return gate_out(o.reshape(B * S, E), u, 2, x.reshape(B * S, D), w_out).reshape(B, S, D)


def hgrn_layer(x, norm, w_in, lb_fwd, lb_bwd, o_norm, w_out):
    B, S, D = x.shape
    E = w_out.shape[0]
    u = norm_matmul(x.reshape(B * S, D), norm, w_in)
    u3 = u.reshape(B, S, 5 * E)
    o_fwd = hgrn_direction(u3, lb_fwd, 1, reverse=False)
    o = hgrn_direction(u3, lb_bwd, 2, reverse=True, prev=o_fwd, o_norm=o_norm)
    return gate_out(o.reshape(B * S, E), u, 4, x.reshape(B * S, D), w_out).reshape(B, S, D)


def rope_lane_tables(positions):
    half = ROPE_DIM // 2
    inv = 1.0 / (ROPE_THETA ** (jnp.arange(0, ROPE_DIM, 2, dtype=F32) / ROPE_DIM))
    ang = positions.astype(F32).reshape(-1, 1) * inv
    cos, sin = jnp.cos(ang), jnp.sin(ang)
    T = ang.shape[0]
    rest = LANES - ROPE_DIM
    cf = jnp.concatenate([cos, cos, jnp.ones((T, rest), F32)], axis=-1)
    sa = jnp.concatenate([-sin, jnp.zeros((T, LANES - half), F32)], axis=-1)
    sb = jnp.concatenate([jnp.zeros((T, half), F32), sin, jnp.zeros((T, rest), F32)], axis=-1)
    return cf, sa, sb


def diff_layer(x, rope, norm, w_in, q_norm, k_norm, lam_q1, lam_k1, lam_q2, lam_k2, sub_norm, w_out, lam_init):
    B, S, D = x.shape
    E = w_out.shape[0]
    u = norm_matmul(x.reshape(B * S, D), norm, w_in)
    qp, kp, vp = attn_prep(u, *rope, q_norm, k_norm)
    lam = jnp.exp(jnp.sum(lam_q1 * lam_k1)) - jnp.exp(jnp.sum(lam_q2 * lam_k2)) + lam_init
    shp = (B, S, E)
    o = diff_attention_core(qp.reshape(shp), kp.reshape(shp), vp.reshape(shp), lam, sub_norm, 1.0 - lam_init)
    return gate_out(o.reshape(B * S, E), u, 3, x.reshape(B * S, D), w_out).reshape(B, S, D)


def fnet_layer(x, norm, w_in, group_w, w_out):
    B, S, D = x.shape
    E = w_out.shape[0]
    u = norm_matmul(x.reshape(B * S, D), norm, w_in)
    y = fourier_core(u.reshape(B, S, 2 * E), group_w)
    return gate_out(y.reshape(B * S, E), u, 1, x.reshape(B * S, D), w_out).reshape(B, S, D)


def hgrn_lower_bounds(table):
    lb = jnp.cumsum(jax.nn.softmax(table.astype(F32), axis=0), axis=0)
    return lb - lb[0:1]


def kernel(x, positions, conv_norm, conv_w_in, conv_dw, conv_dw_b, conv_ln_g, conv_ln_b, conv_w_out, hgrn_norm, hgrn_w_in, hgrn_lb_fwd, hgrn_lb_bwd, hgrn_o_norm, hgrn_w_out, diff_norm, diff_w_in, diff_q_norm, diff_k_norm, diff_lam_q1, diff_lam_k1, diff_lam_q2, diff_lam_k2, diff_sub_norm, diff_w_out, fnet_norm, fnet_w_in, fnet_group_w, fnet_w_out):
    depth = hgrn_lb_fwd.shape[0]
    n_mixers = 4
    rope = rope_lane_tables(positions)
    lb_fwd = hgrn_lower_bounds(hgrn_lb_fwd)
    lb_bwd = hgrn_lower_bounds(hgrn_lb_bwd)
    for layer in range(depth):
        m, j = layer % n_mixers, layer // n_mixers
        if m == 0:
            x = conv_layer(x, conv_norm[j], conv_w_in[j], conv_dw[j], conv_dw_b[j], conv_ln_g[j], conv_ln_b[j],
                           conv_w_out[j])
        elif m == 1:
            x = hgrn_layer(x, hgrn_norm[j], hgrn_w_in[j], lb_fwd[layer], lb_bwd[layer], hgrn_o_norm[j],
                           hgrn_w_out[j])
        elif m == 2:
            lam_init = 0.8 - 0.6 * math.exp(-0.3 * layer)
            x = diff_layer(x, rope, diff_norm[j], diff_w_in[j], diff_q_norm[j], diff_k_norm[j], diff_lam_q1[j],
                           diff_lam_k1[j], diff_lam_q2[j], diff_lam_k2[j], diff_sub_norm[j], diff_w_out[j], lam_init)
        else:
            x = fnet_layer(x, fnet_norm[j], fnet_w_in[j], fnet_group_w[j], fnet_w_out[j])
    return x
```

```python
import functools
import math

import numpy as np
import jax
import jax.numpy as jnp
from jax import lax
from jax.experimental import pallas as pl
from jax.experimental.pallas import tpu as pltpu

EPS = 1e-6
LANES = 128
HEAD_DIM = 128
CONV_HALO = 16
ROPE_DIM = 32
ROPE_THETA = 500000.0
FN_GROUPS = 8
MIB = 1 << 20

F32 = jnp.float32
BF16 = jnp.bfloat16
NT_DIMS = (((1,), (1,)), ((), ()))


def _params(semantics, vmem_mib):
    return pltpu.CompilerParams(dimension_semantics=semantics, vmem_limit_bytes=vmem_mib * MIB)


def _silu(z):
    return z * jax.nn.sigmoid(z)


def _norm_matmul_kernel(x_ref, g_ref, w_ref, o_ref, h_ref):
    @pl.when(pl.program_id(1) == 0)
    def _():
        x = x_ref[...]
        ms = jnp.mean(x * x, axis=-1, keepdims=True)
        h_ref[...] = (x * lax.rsqrt(ms + EPS) * g_ref[...]).astype(h_ref.dtype)

    o_ref[...] = jnp.dot(h_ref[...], w_ref[...], preferred_element_type=F32).astype(o_ref.dtype)


def norm_matmul(x2d, gain, w, *, tm=1024, tn=1024):
    T, D = x2d.shape
    N = w.shape[1]
    tm, tn = min(tm, T), min(tn, N)
    return pl.pallas_call(
        _norm_matmul_kernel,
        grid=(T // tm, N // tn),
        in_specs=[
            pl.BlockSpec((tm, D), lambda i, j: (i, 0)),
            pl.BlockSpec((1, D), lambda i, j: (0, 0)),
            pl.BlockSpec((D, tn), lambda i, j: (0, j)),
        ],
        out_specs=pl.BlockSpec((tm, tn), lambda i, j: (i, j)),
        out_shape=jax.ShapeDtypeStruct((T, N), F32),
        scratch_shapes=[pltpu.VMEM((tm, D), BF16)],
        compiler_params=_params(("parallel", "arbitrary"), 48),
        name="norm_matmul",
    )(x2d, gain.reshape(1, D), w.astype(BF16))


def _gate_out_kernel(o_ref, z_ref, x_ref, w_ref, out_ref):
    g = (o_ref[...] * _silu(z_ref[...])).astype(BF16)
    out_ref[...] = x_ref[...] + jnp.dot(g, w_ref[...], preferred_element_type=F32)


def gate_out(o2d, u2d, z_block, x2d, w_out, *, tm=256):
    T, E = o2d.shape
    D = w_out.shape[1]
    tm = min(tm, T)
    return pl.pallas_call(
        _gate_out_kernel,
        grid=(T // tm,),
        in_specs=[
            pl.BlockSpec((tm, E), lambda i: (i, 0)),
            pl.BlockSpec((tm, E), lambda i: (i, z_block)),
            pl.BlockSpec((tm, D), lambda i: (i, 0)),
            pl.BlockSpec((E, D), lambda i: (0, 0)),
        ],
        out_specs=pl.BlockSpec((tm, D), lambda i: (i, 0)),
        out_shape=jax.ShapeDtypeStruct((T, D), F32),
        compiler_params=_params(("parallel",), 48),
        name="gate_out",
    )(o2d, u2d, x2d, w_out.astype(BF16))


def _conv_kernel(am_ref, ap_ref, an_ref, bm_ref, bp_ref, bn_ref, dw_ref, dwb_ref, g_ref, beta_ref,
                 o_ref, vbuf, *, taps, row_block, col_block):
    s = pl.program_id(1)
    ts, E = am_ref.shape[1], am_ref.shape[2]
    glu = lambda a, b: a * jax.nn.sigmoid(b)
    vbuf[0:CONV_HALO, :] = jnp.where(s > 0, glu(ap_ref[0], bp_ref[0]), 0.0)
    vbuf[CONV_HALO:CONV_HALO + ts, :] = glu(am_ref[0], bm_ref[0])
    vbuf[CONV_HALO + ts:, :] = jnp.where(s < pl.num_programs(1) - 1, glu(an_ref[0], bn_ref[0]), 0.0)
    first = CONV_HALO - taps // 2
    for r0 in range(0, ts, row_block):
        for c0 in range(0, E, col_block):
            cols = slice(c0, c0 + col_block)
            acc = jnp.zeros((row_block, col_block), F32)
            for k in range(taps):
                acc = acc + dw_ref[k:k + 1, cols] * vbuf[r0 + first + k:r0 + first + k + row_block, cols]
            o_ref[0, r0:r0 + row_block, cols] = acc + dwb_ref[:, cols]
    c = o_ref[0]
    mu = jnp.mean(c, axis=-1, keepdims=True)
    cc = c - mu
    var = jnp.mean(cc * cc, axis=-1, keepdims=True)
    y = cc * lax.rsqrt(var + EPS) * g_ref[...] + beta_ref[...]
    o_ref[0] = _silu(y)


def conv_mixer(u, dw, dw_b, ln_g, ln_b, *, ts=256):
    B, S, N = u.shape
    E = N // 3
    taps = dw.shape[0]
    ts = min(ts, S)
    hb = ts // CONV_HALO
    last_halo = S // CONV_HALO - 1
    main = lambda col: pl.BlockSpec((1, ts, E), lambda b, s: (b, s, col))
    prev = lambda col: pl.BlockSpec((1, CONV_HALO, E), lambda b, s: (b, jnp.maximum(s * hb - 1, 0), col))
    nxt = lambda col: pl.BlockSpec((1, CONV_HALO, E), lambda b, s: (b, jnp.minimum((s + 1) * hb, last_halo), col))
    vec = lambda rows: pl.BlockSpec((rows, E), lambda b, s: (0, 0))
    return pl.pallas_call(
        functools.partial(_conv_kernel, taps=taps, row_block=32, col_block=512),
        grid=(B, S // ts),
        in_specs=[main(0), prev(0), nxt(0), main(1), prev(1), nxt(1), vec(taps), vec(1), vec(1), vec(1)],
        out_specs=pl.BlockSpec((1, ts, E), lambda b, s: (b, s, 0)),
        out_shape=jax.ShapeDtypeStruct((B, S, E), F32),
        scratch_shapes=[pltpu.VMEM((ts + 2 * CONV_HALO, E), F32)],
        compiler_params=_params(("parallel", "parallel"), 40),
        name="conv_mixer",
    )(u, u, u, u, u, u, dw, dw_b.reshape(1, E), ln_g.reshape(1, E), ln_b.reshape(1, E))


def _hgrn_kernel(*refs, reverse, finalize, chunk):
    if finalize:
        q_ref, a_ref, v_ref, lb_ref, prev_ref, gn_ref, o_ref, st_ref = refs
    else:
        q_ref, a_ref, v_ref, lb_ref, o_ref, st_ref = refs
    C = chunk
    n = q_ref.shape[1] // C
    heads = q_ref.shape[2] // HEAD_DIM

    @pl.when(pl.program_id(2) == 0)
    def _():
        st_ref[...] = jnp.zeros_like(st_ref)

    row = lax.broadcasted_iota(jnp.int32, (C, HEAD_DIM), 0)
    pair_xor = lax.broadcasted_iota(jnp.int32, (C, C), 0) ^ lax.broadcasted_iota(jnp.int32, (C, C), 1)
    pair_level = (pltpu.bitcast(pair_xor.astype(F32), jnp.int32) >> 23) - 127

    def one_chunk(j, carry):
        r0 = pl.multiple_of((n - 1 - j if reverse else j) * C, C)
        rows = pl.ds(r0, C)
        q, kk, v, near, far, tot, scores = [], [], [], [], [], [], []
        for h in range(heads):
            lanes = slice(h * HEAD_DIM, (h + 1) * HEAD_DIM)
            a, lb = a_ref[0, rows, lanes], lb_ref[:, lanes]
            e = jnp.exp(-jnp.abs(a))
            r = 1.0 / (1.0 + e)
            er = e * r
            pos = a >= 0
            f = lb + (1.0 - lb) * jnp.where(pos, r, er)
            q.append(q_ref[0, rows, lanes])
            v.append(v_ref[0, rows, lanes])
            kk.append((1.0 - lb) * jnp.where(pos, er, r))
            near.append(f)
            far.append(jnp.ones_like(f))
            tot.append(f)
            scores.append(jnp.zeros((C, C), F32))

        m, level = 1, 0
        while m < C:
            bit = (row & m) != 0
            q_rows = jnp.logical_not(bit) if reverse else bit
            for h in range(heads):
                qm = jnp.where(q_rows, q[h] * near[h], 0.0).astype(BF16)
                km = (kk[h] * far[h]).astype(BF16)
                sc = lax.dot_general(qm, km, NT_DIMS, preferred_element_type=F32)
                scores[h] = jnp.where(pair_level == level, sc, scores[h])
                tot_before = pltpu.roll(tot[h], m, 0)
                tot_after = pltpu.roll(tot[h], C - m, 0)
                if reverse:
                    near[h] = jnp.where(bit, near[h], near[h] * tot_after)
                    far[h] = jnp.where(bit, far[h] * tot_before, far[h])
                else:
                    near[h] = jnp.where(bit, near[h] * tot_before, near[h])
                    far[h] = jnp.where(bit, far[h], far[h] * tot_after)
                tot[h] = tot[h] * jnp.where(bit, tot_before, tot_after)
            m *= 2
            level += 1

        for h in range(heads):
            lanes = slice(h * HEAD_DIM, (h + 1) * HEAD_DIM)
            vb = v[h].astype(BF16)
            st = st_ref[h]
            o = jnp.dot(scores[h].astype(BF16), vb, preferred_element_type=F32)
            o = o + jnp.sum(q[h] * kk[h], axis=1, keepdims=True) * v[h]
            o = o + lax.dot_general((q[h] * near[h]).astype(BF16), st.astype(BF16), NT_DIMS,
                                    preferred_element_type=F32)
            st_ref[h] = st * tot[h][0:1] + jnp.dot(v[h].T.astype(BF16), (kk[h] * far[h]).astype(BF16),
                                                   preferred_element_type=F32)
            if finalize:
                o = o + prev_ref[0, rows, lanes]
                ms = jnp.mean(o * o, axis=-1, keepdims=True)
                o = o * lax.rsqrt(ms + EPS) * gn_ref[...]
            o_ref[0, rows, lanes] = o
        return carry

    lax.fori_loop(0, n, one_chunk, 0)


def hgrn_direction(u, lb, gate_block, *, reverse, prev=None, o_norm=None, rows=1024, chunk=128, heads=4):
    B, S, N = u.shape
    E = N // 5
    W = heads * HEAD_DIM
    HB = E // W
    R = min(rows, S)
    nb = S // R
    finalize = prev is not None
    ridx = (lambda c: nb - 1 - c) if reverse else (lambda c: c)
    col = lambda blk: pl.BlockSpec((1, R, W), lambda b, h, c: (b, ridx(c), blk * HB + h))
    in_specs = [col(0), col(gate_block), col(3), pl.BlockSpec((1, W), lambda b, h, c: (0, h))]
    args = [u, u, u, lb.reshape(1, E)]
    if finalize:
        in_specs += [pl.BlockSpec((1, R, W), lambda b, h, c: (b, ridx(c), h)),
                     pl.BlockSpec((1, HEAD_DIM), lambda b, h, c: (0, 0))]
        args += [prev, o_norm.reshape(1, HEAD_DIM)]
    return pl.pallas_call(
        functools.partial(_hgrn_kernel, reverse=reverse, finalize=finalize, chunk=min(chunk, R)),
        grid=(B, HB, nb),
        in_specs=in_specs,
        out_specs=pl.BlockSpec((1, R, W), lambda b, h, c: (b, ridx(c), h)),
        out_shape=jax.ShapeDtypeStruct((B, S, E), F32),
        scratch_shapes=[pltpu.VMEM((heads, HEAD_DIM, HEAD_DIM), F32)],
        compiler_params=_params(("parallel", "parallel", "arbitrary"), 40),
        name="hgrn_bwd" if reverse else "hgrn_fwd",
    )(*args)


def _attn_prep_kernel(q_ref, k_ref, v_ref, cf_ref, sa_ref, sb_ref, qn_ref, kn_ref, qt_ref, ko_ref, vt_ref,
                      *, q_scale):
    cf, sa, sb = cf_ref[0], sa_ref[0], sb_ref[0]
    half = ROPE_DIM // 2

    def norm_rope(x, gain):
        ms = jnp.mean(x * x, axis=-1, keepdims=True)
        xn = x * lax.rsqrt(ms + EPS) * gain
        return xn * cf + pltpu.roll(xn, LANES - half, 1) * sa + pltpu.roll(xn, half, 1) * sb

    for g in range(q_ref.shape[2] // HEAD_DIM):
        sl = slice(g * HEAD_DIM, (g + 1) * HEAD_DIM)
        qt_ref[0, sl, :] = (norm_rope(q_ref[0, :, sl], qn_ref[...]) * q_scale).T.astype(BF16)
        ko_ref[0, :, sl] = norm_rope(k_ref[0, :, sl], kn_ref[...]).astype(BF16)
        vt_ref[0, sl, :] = v_ref[0, :, sl].T.astype(BF16)


def attn_prep(u, rope_cf, rope_sa, rope_sb, q_norm, k_norm, *, tm=256):
    B, S, N = u.shape
    E = N // 4
    tm = min(tm, S)
    col = lambda blk: pl.BlockSpec((1, tm, E), lambda b, i: (b, i, blk))
    tab = pl.BlockSpec((1, tm, LANES), lambda b, i: (b, i, 0))
    gain = pl.BlockSpec((1, HEAD_DIM), lambda b, i: (0, 0))
    rows = pl.BlockSpec((1, tm, E), lambda b, i: (b, i, 0))
    cols = pl.BlockSpec((1, E, tm), lambda b, i: (b, 0, i))
    tables = [t.reshape(B, S, LANES) for t in (rope_cf, rope_sa, rope_sb)]
    return pl.pallas_call(
        functools.partial(_attn_prep_kernel, q_scale=math.log2(math.e) / math.sqrt(HEAD_DIM)),
        grid=(B, S // tm),
        in_specs=[col(0), col(1), col(2), tab, tab, tab, gain, gain],
        out_specs=[cols, rows, cols],
        out_shape=[jax.ShapeDtypeStruct((B, E, S), BF16), jax.ShapeDtypeStruct((B, S, E), BF16),
                   jax.ShapeDtypeStruct((B, E, S), BF16)],
        compiler_params=_params(("parallel", "parallel"), 48),
        name="attn_prep",
    )(u, u, u, *tables, q_norm.reshape(1, HEAD_DIM), k_norm.reshape(1, HEAD_DIM))


def _attn_kernel(lam_ref, qt_ref, k_ref, vt_ref, sn_ref, o_ref, m_ref, l_ref, acc_ref, s00, s01, s10, s11, mc_ref,
                 *, post_scale, tkc):
    n = k_ref.shape[1] // tkc
    s_refs = ((s00, s01), (s10, s11))
    m_ref[...] = jnp.full_like(m_ref, -jnp.inf)
    l_ref[...] = jnp.zeros_like(l_ref)
    acc_ref[...] = jnp.zeros_like(acc_ref)

    def produce(c, slot, i):
        sl = slice(i * HEAD_DIM, (i + 1) * HEAD_DIM)
        off = pl.multiple_of(c * tkc, tkc)
        s_new = jnp.dot(k_ref[0, pl.ds(off, tkc), sl], qt_ref[0, sl, :], preferred_element_type=F32)
        s_refs[slot][i][...] = s_new
        mc_ref[slot, i] = jnp.max(s_new, axis=0, keepdims=True)

    def consume(vt_c, slot, i):
        m_prev = m_ref[i]
        m_new = jnp.maximum(m_prev, mc_ref[slot, i])
        alpha = jnp.exp2(m_prev - m_new)
        p = jnp.exp2(s_refs[slot][i][...] - m_new)
        l_ref[i] = alpha * l_ref[i] + jnp.sum(p, axis=0, keepdims=True)
        acc_ref[i] = alpha * acc_ref[i] + jnp.dot(vt_c, p.astype(BF16), preferred_element_type=F32)
        m_ref[i] = m_new

    def step(c, slot, make_next=True):
        vt_c = vt_ref[0, :, pl.ds(pl.multiple_of(c * tkc, tkc), tkc)]
        for i in range(2):
            if make_next:
                produce(c + 1, 1 - slot, i)
            consume(vt_c, slot, i)

    produce(0, 0, 0)
    produce(0, 0, 1)

    def pair(j, carry):
        step(2 * j, 0)
        step(2 * j + 1, 1)
        return carry

    lax.fori_loop(0, n // 2 - 1, pair, 0)
    step(n - 2, 0)
    step(n - 1, 1, make_next=False)
    o = (acc_ref[0] / l_ref[0] - lam_ref[0] * (acc_ref[1] / l_ref[1])).T
    ms = jnp.mean(o * o, axis=-1, keepdims=True)
    o_ref[0] = o * lax.rsqrt(ms + EPS) * sn_ref[...] * post_scale


def diff_attention_core(qt, kp, vt, lam, sub_norm, post_scale, *, tq=512, tkc=1024):
    B, S, E = kp.shape
    W = 2 * HEAD_DIM
    H = E // W
    tq, tkc = min(tq, S), min(tkc, S // 2)
    assert S % (2 * tkc) == 0
    return pl.pallas_call(
        functools.partial(_attn_kernel, post_scale=post_scale, tkc=tkc),
        grid=(B, H, S // tq),
        in_specs=[
            pl.BlockSpec(memory_space=pltpu.SMEM),
            pl.BlockSpec((1, W, tq), lambda b, h, i: (b, h, i)),
            pl.BlockSpec((1, S, W), lambda b, h, i: (b, 0, h)),
            pl.BlockSpec((1, W, S), lambda b, h, i: (b, h, 0)),
            pl.BlockSpec((1, W), lambda b, h, i: (0, 0)),
        ],
        out_specs=pl.BlockSpec((1, tq, W), lambda b, h, i: (b, i, h)),
        out_shape=jax.ShapeDtypeStruct((B, S, E), F32),
        scratch_shapes=[pltpu.VMEM((2, 1, tq), F32), pltpu.VMEM((2, 1, tq), F32), pltpu.VMEM((2, W, tq), F32)]
        + [pltpu.VMEM((tkc, tq), F32)] * 4 + [pltpu.VMEM((2, 2, 1, tq), F32)],
        compiler_params=_params(("parallel", "parallel", "arbitrary"), 56),
        name="diff_attention",
    )(lam.reshape(1), qt, kp, vt, sub_norm.reshape(1, W))


def _dft_tables(n):
    ang = 2.0 * np.pi * ((np.arange(n)[:, None] * np.arange(n)[None, :]) % n) / n
    return jnp.asarray(np.cos(ang), F32), jnp.asarray(np.sin(ang), F32)


def _fnet_a_kernel(x_ref, c_ref, s_ref, tc_ref, ts_ref, ar_ref, ai_ref):
    x = x_ref[0].astype(BF16)
    ar = jnp.dot(c_ref[...], x, preferred_element_type=F32)
    ai = -jnp.dot(s_ref[...], x, preferred_element_type=F32)
    tc, ts = tc_ref[0], ts_ref[0]
    ar_ref[0] = ar * tc + ai * ts
    ai_ref[0] = ai * tc - ar * ts


def _fnet_b_kernel(ar_ref, ai_ref, c_ref, s_ref, cc_ref, sc_ref, gw_ref, y_ref, *, norm):
    ar, ai = ar_ref[0].astype(BF16), ai_ref[0].astype(BF16)
    c, s = c_ref[...], s_ref[...]
    br = jnp.dot(c, ar, preferred_element_type=F32) + jnp.dot(s, ai, preferred_element_type=F32)
    bi = jnp.dot(c, ai, preferred_element_type=F32) - jnp.dot(s, ar, preferred_element_type=F32)
    G = gw_ref.shape[1]
    for g in range(gw_ref.shape[0]):
        sl = slice(g * G, (g + 1) * G)
        f = (jnp.dot(br[:, sl].astype(BF16), cc_ref[...], preferred_element_type=F32)
             + jnp.dot(bi[:, sl].astype(BF16), sc_ref[...], preferred_element_type=F32)) * norm
        y_ref[0, :, sl] = jnp.dot(f.astype(BF16), gw_ref[g], preferred_element_type=F32)


def fourier_core(u, group_w, *, n_lo=128):
    B, S, N = u.shape
    E = N // 2
    G = E // FN_GROUPS
    n_hi = S // n_lo
    c_hi, s_hi = _dft_tables(n_hi)
    c_lo, s_lo = _dft_tables(n_lo)
    c_ch, s_ch = _dft_tables(G)
    ang = 2.0 * np.pi * ((np.arange(n_lo)[:, None] * np.arange(n_hi)[None, :]) % S) / S
    tw_c = jnp.asarray(np.cos(ang), F32)[:, :, None]
    tw_s = jnp.asarray(np.sin(ang), F32)[:, :, None]
    mat = lambda n: pl.BlockSpec((n, n), lambda b, j: (0, 0))
    tw = pl.BlockSpec((1, n_hi, 1), lambda b, j: (j, 0, 0))
    a_out = pl.BlockSpec((1, n_hi, E), lambda b, j: (b, 0, j))
    ar, ai = pl.pallas_call(
        _fnet_a_kernel,
        grid=(B, n_lo),
        in_specs=[pl.BlockSpec((1, n_hi, E), lambda b, j: (b, 0, 2 * j)), mat(n_hi), mat(n_hi), tw, tw],
        out_specs=[a_out, a_out],
        out_shape=[jax.ShapeDtypeStruct((B, n_hi, n_lo * E), F32)] * 2,
        compiler_params=_params(("parallel", "parallel"), 32),
        name="fnet_stage_a",
    )(u.reshape(B, n_hi, n_lo * N), c_hi.astype(BF16), s_hi.astype(BF16), tw_c, tw_s)

    a_in = pl.BlockSpec((1, n_lo, E), lambda b, j: (b * n_hi + j, 0, 0))
    y = pl.pallas_call(
        functools.partial(_fnet_b_kernel, norm=1.0 / math.sqrt(S * G)),
        grid=(B, n_hi),
        in_specs=[a_in, a_in, mat(n_lo), mat(n_lo), mat(G), mat(G),
                  pl.BlockSpec((FN_GROUPS, G, G), lambda b, j: (0, 0, 0))],
        out_specs=pl.BlockSpec((1, n_lo, E), lambda b, j: (b, 0, j)),
        out_shape=jax.ShapeDtypeStruct((B, n_lo, n_hi * E), F32),
        compiler_params=_params(("parallel", "parallel"), 32),
        name="fnet_stage_b",
    )(ar.reshape(B * n_hi, n_lo, E), ai.reshape(B * n_hi, n_lo, E), c_lo.astype(BF16), s_lo.astype(BF16),
      c_ch.astype(BF16), s_ch.astype(BF16), group_w.astype(BF16))
    return y.reshape(B, S, E)


def conv_layer(x, norm, w_in, dw, dw_b, ln_g, ln_b, w_out):
    B, S, D = x.shape
    E = w_out.shape[0]
    u = norm_matmul(x.reshape(B * S, D), norm, w_in)
    o = conv_mixer(u.reshape(B, S, 3 * E), dw, dw_b, ln_g, ln_b)
    return gate_out(o.reshape(B * S, E), u, 2, x.reshape(B * S, D), w_out).reshape(B, S, D)


def hgrn_layer(x, norm, w_in, lb_fwd, lb_bwd, o_norm, w_out):
    B, S, D = x.shape
    E = w_out.shape[0]
    u = norm_matmul(x.reshape(B * S, D), norm, w_in)
    u3 = u.reshape(B, S, 5 * E)
    o_fwd = hgrn_direction(u3, lb_fwd, 1, reverse=False)
    o = hgrn_direction(u3, lb_bwd, 2, reverse=True, prev=o_fwd, o_norm=o_norm)
    return gate_out(o.reshape(B * S, E), u, 4, x.reshape(B * S, D), w_out).reshape(B, S, D)


def rope_lane_tables(positions):
    half = ROPE_DIM // 2
    inv = 1.0 / (ROPE_THETA ** (jnp.arange(0, ROPE_DIM, 2, dtype=F32) / ROPE_DIM))
    ang = positions.astype(F32).reshape(-1, 1) * inv
    cos, sin = jnp.cos(ang), jnp.sin(ang)
    T = ang.shape[0]
    rest = LANES - ROPE_DIM
    cf = jnp.concatenate([cos, cos, jnp.ones((T, rest), F32)], axis=-1)
    sa = jnp.concatenate([-sin, jnp.zeros((T, LANES - half), F32)], axis=-1)
    sb = jnp.concatenate([jnp.zeros((T, half), F32), sin, jnp.zeros((T, rest), F32)], axis=-1)
    return cf, sa, sb


def diff_layer(x, rope, norm, w_in, q_norm, k_norm, lam_q1, lam_k1, lam_q2, lam_k2, sub_norm, w_out, lam_init):
    B, S, D = x.shape
    E = w_out.shape[0]
    u = norm_matmul(x.reshape(B * S, D), norm, w_in)
    qt, kp, vt = attn_prep(u.reshape(B, S, 4 * E), *rope, q_norm, k_norm)
    lam = jnp.exp(jnp.sum(lam_q1 * lam_k1)) - jnp.exp(jnp.sum(lam_q2 * lam_k2)) + lam_init
    o = diff_attention_core(qt, kp, vt, lam, sub_norm, 1.0 - lam_init)
    return gate_out(o.reshape(B * S, E), u, 3, x.reshape(B * S, D), w_out).reshape(B, S, D)


def fnet_layer(x, norm, w_in, group_w, w_out):
    B, S, D = x.shape
    E = w_out.shape[0]
    u = norm_matmul(x.reshape(B * S, D), norm, w_in)
    y = fourier_core(u.reshape(B, S, 2 * E), group_w)
    return gate_out(y.reshape(B * S, E), u, 1, x.reshape(B * S, D), w_out).reshape(B, S, D)


def hgrn_lower_bounds(table):
    lb = jnp.cumsum(jax.nn.softmax(table.astype(F32), axis=0), axis=0)
    return lb - lb[0:1]


def kernel(x, positions, conv_norm, conv_w_in, conv_dw, conv_dw_b, conv_ln_g, conv_ln_b, conv_w_out, hgrn_norm, hgrn_w_in, hgrn_lb_fwd, hgrn_lb_bwd, hgrn_o_norm, hgrn_w_out, diff_norm, diff_w_in, diff_q_norm, diff_k_norm, diff_lam_q1, diff_lam_k1, diff_lam_q2, diff_lam_k2, diff_sub_norm, diff_w_out, fnet_norm, fnet_w_in, fnet_group_w, fnet_w_out):
    depth = hgrn_lb_fwd.shape[0]
    n_mixers = 4
    rope = rope_lane_tables(positions)
    lb_fwd = hgrn_lower_bounds(hgrn_lb_fwd)
    lb_bwd = hgrn_lower_bounds(hgrn_lb_bwd)
    for layer in range(depth):
        m, j = layer % n_mixers, layer // n_mixers
        if m == 0:
            x = conv_layer(x, conv_norm[j], conv_w_in[j], conv_dw[j], conv_dw_b[j], conv_ln_g[j], conv_ln_b[j],
                           conv_w_out[j])
        elif m == 1:
            x = hgrn_layer(x, hgrn_norm[j], hgrn_w_in[j], lb_fwd[layer], lb_bwd[layer], hgrn_o_norm[j],
                           hgrn_w_out[j])
        elif m == 2:
            lam_init = 0.8 - 0.6 * math.exp(-0.3 * layer)
            x = diff_layer(x, rope, diff_norm[j], diff_w_in[j], diff_q_norm[j], diff_k_norm[j], diff_lam_q1[j],
                           diff_lam_k1[j], diff_lam_q2[j], diff_lam_k2[j], diff_sub_norm[j], diff_w_out[j], lam_init)
        else:
            x = fnet_layer(x, fnet_norm[j], fnet_w_in[j], fnet_group_w[j], fnet_w_out[j])
    return x
```

```python
import functools
import math

import numpy as np
import jax
import jax.numpy as jnp
from jax import lax
from jax.experimental import pallas as pl
from jax.experimental.pallas import tpu as pltpu

EPS = 1e-6
LANES = 128
SUBLANES = 8
HEAD_DIM = 128
CONV_HALO = 16
ROPE_DIM = 32
ROPE_THETA = 500000.0
FN_GROUPS = 8
MIB = 1 << 20

F32 = jnp.float32
BF16 = jnp.bfloat16
NT_DIMS = (((1,), (1,)), ((), ()))


def _params(semantics, vmem_mib):
    return pltpu.CompilerParams(dimension_semantics=semantics, vmem_limit_bytes=vmem_mib * MIB)


def _silu(z):
    return z * jax.nn.sigmoid(z)


def _norm_matmul_kernel(x_ref, g_ref, w_ref, o_ref, h_ref):
    @pl.when(pl.program_id(1) == 0)
    def _():
        x = x_ref[...]
        ms = jnp.mean(x * x, axis=-1, keepdims=True)
        h_ref[...] = (x * lax.rsqrt(ms + EPS) * g_ref[...]).astype(h_ref.dtype)

    o_ref[...] = jnp.dot(h_ref[...], w_ref[...], preferred_element_type=F32).astype(o_ref.dtype)


def norm_matmul(x2d, gain, w, *, tm=1024, tn=1024):
    T, D = x2d.shape
    N = w.shape[1]
    tm, tn = min(tm, T), min(tn, N)
    return pl.pallas_call(
        _norm_matmul_kernel,
        grid=(T // tm, N // tn),
        in_specs=[
            pl.BlockSpec((tm, D), lambda i, j: (i, 0)),
            pl.BlockSpec((1, D), lambda i, j: (0, 0)),
            pl.BlockSpec((D, tn), lambda i, j: (0, j)),
        ],
        out_specs=pl.BlockSpec((tm, tn), lambda i, j: (i, j)),
        out_shape=jax.ShapeDtypeStruct((T, N), F32),
        scratch_shapes=[pltpu.VMEM((tm, D), BF16)],
        compiler_params=_params(("parallel", "arbitrary"), 48),
        name="norm_matmul",
    )(x2d, gain.reshape(1, D), w.astype(BF16))


def _gate_out_kernel(o_ref, z_ref, x_ref, w_ref, out_ref):
    g = (o_ref[...] * _silu(z_ref[...])).astype(BF16)
    out_ref[...] = x_ref[...] + jnp.dot(g, w_ref[...], preferred_element_type=F32)


def gate_out(o2d, u2d, z_block, x2d, w_out, *, tm=256):
    T, E = o2d.shape
    D = w_out.shape[1]
    tm = min(tm, T)
    return pl.pallas_call(
        _gate_out_kernel,
        grid=(T // tm,),
        in_specs=[
            pl.BlockSpec((tm, E), lambda i: (i, 0)),
            pl.BlockSpec((tm, E), lambda i: (i, z_block)),
            pl.BlockSpec((tm, D), lambda i: (i, 0)),
            pl.BlockSpec((E, D), lambda i: (0, 0)),
        ],
        out_specs=pl.BlockSpec((tm, D), lambda i: (i, 0)),
        out_shape=jax.ShapeDtypeStruct((T, D), F32),
        compiler_params=_params(("parallel",), 48),
        name="gate_out",
    )(o2d, u2d, x2d, w_out.astype(BF16))


def _conv_kernel(am_ref, ap_ref, an_ref, bm_ref, bp_ref, bn_ref, dw_ref, dwb_ref, g_ref, beta_ref,
                 o_ref, vbuf, shifted, cbuf, *, taps, row_block, col_block):
    s = pl.program_id(1)
    ts, E = am_ref.shape[1], am_ref.shape[2]
    glu = lambda a, b: a * jax.nn.sigmoid(b)
    vbuf[0:CONV_HALO, :] = jnp.where(s > 0, glu(ap_ref[0], bp_ref[0]), 0.0)
    vbuf[CONV_HALO:CONV_HALO + ts, :] = glu(am_ref[0], bm_ref[0])
    vbuf[CONV_HALO + ts:, :] = jnp.where(s < pl.num_programs(1) - 1, glu(an_ref[0], bn_ref[0]), 0.0)
    span = shifted.shape[1]
    for r in range(1, SUBLANES):
        shifted[r - 1] = vbuf[r:r + span, :]
    first = CONV_HALO - taps // 2
    for r0 in range(0, ts, row_block):
        for c0 in range(0, E, col_block):
            cols = slice(c0, c0 + col_block)
            acc = jnp.zeros((row_block, col_block), F32)
            for k in range(taps):
                phase, base = (first + k) % SUBLANES, r0 + (first + k) // SUBLANES * SUBLANES
                win = (vbuf[base:base + row_block, cols] if phase == 0
                       else shifted[phase - 1, base:base + row_block, cols])
                acc = acc + dw_ref[k:k + 1, cols] * win
            cbuf[r0:r0 + row_block, cols] = acc + dwb_ref[:, cols]
    c = cbuf[...]
    mu = jnp.mean(c, axis=-1, keepdims=True)
    cc = c - mu
    var = jnp.mean(cc * cc, axis=-1, keepdims=True)
    y = cc * lax.rsqrt(var + EPS) * g_ref[...] + beta_ref[...]
    o_ref[0] = _silu(y).astype(o_ref.dtype)


def conv_mixer(u, dw, dw_b, ln_g, ln_b, *, ts=256):
    B, S, N = u.shape
    E = N // 3
    taps = dw.shape[0]
    ts = min(ts, S)
    hb = ts // CONV_HALO
    last_halo = S // CONV_HALO - 1
    main = lambda col: pl.BlockSpec((1, ts, E), lambda b, s: (b, s, col))
    prev = lambda col: pl.BlockSpec((1, CONV_HALO, E), lambda b, s: (b, jnp.maximum(s * hb - 1, 0), col))
    nxt = lambda col: pl.BlockSpec((1, CONV_HALO, E), lambda b, s: (b, jnp.minimum((s + 1) * hb, last_halo), col))
    vec = lambda rows: pl.BlockSpec((rows, E), lambda b, s: (0, 0))
    return pl.pallas_call(
        functools.partial(_conv_kernel, taps=taps, row_block=32, col_block=512),
        grid=(B, S // ts),
        in_specs=[main(0), prev(0), nxt(0), main(1), prev(1), nxt(1), vec(taps), vec(1), vec(1), vec(1)],
        out_specs=pl.BlockSpec((1, ts, E), lambda b, s: (b, s, 0)),
        out_shape=jax.ShapeDtypeStruct((B, S, E), BF16),
        scratch_shapes=[pltpu.VMEM((ts + 2 * CONV_HALO, E), F32),
                        pltpu.VMEM((SUBLANES - 1, ts + 2 * CONV_HALO - SUBLANES, E), F32),
                        pltpu.VMEM((ts, E), F32)],
        compiler_params=_params(("parallel", "parallel"), 56),
        name="conv_mixer",
    )(u, u, u, u, u, u, dw, dw_b.reshape(1, E), ln_g.reshape(1, E), ln_b.reshape(1, E))


def _hgrn_kernel(*refs, reverse, finalize, chunk):
    if finalize:
        q_ref, a_ref, v_ref, lb_ref, prev_ref, gn_ref, o_ref, st_ref = refs
    else:
        q_ref, a_ref, v_ref, lb_ref, o_ref, st_ref = refs
    C = chunk
    n = q_ref.shape[1] // C
    heads = q_ref.shape[2] // HEAD_DIM

    @pl.when(pl.program_id(2) == 0)
    def _():
        st_ref[...] = jnp.zeros_like(st_ref)

    row = lax.broadcasted_iota(jnp.int32, (C, HEAD_DIM), 0)
    pair_xor = lax.broadcasted_iota(jnp.int32, (C, C), 0) ^ lax.broadcasted_iota(jnp.int32, (C, C), 1)
    tt, ss = lax.broadcasted_iota(jnp.int32, (C, C), 0), lax.broadcasted_iota(jnp.int32, (C, C), 1)
    pair_level = jnp.where((tt < ss) if reverse else (tt > ss),
                           (pltpu.bitcast(pair_xor.astype(F32), jnp.int32) >> 23) - 127, -1)

    def one_chunk(j, carry):
        r0 = pl.multiple_of((n - 1 - j if reverse else j) * C, C)
        rows = pl.ds(r0, C)
        q, kk, v, near, far, tot, scores = [], [], [], [], [], [], []
        for h in range(heads):
            lanes = slice(h * HEAD_DIM, (h + 1) * HEAD_DIM)
            a, lb = a_ref[0, rows, lanes], lb_ref[:, lanes]
            e = jnp.exp(-jnp.abs(a))
            r = 1.0 / (1.0 + e)
            er = e * r
            pos = a >= 0
            f = lb + (1.0 - lb) * jnp.where(pos, r, er)
            q.append(q_ref[0, rows, lanes])
            v.append(v_ref[0, rows, lanes])
            kk.append((1.0 - lb) * jnp.where(pos, er, r))
            near.append(f)
            far.append(jnp.ones_like(f))
            tot.append(f)
            scores.append(jnp.zeros((C, C), F32))

        m, level = 1, 0
        while m < C:
            bit = (row & m) != 0
            for h in range(heads):
                qm = (q[h] * near[h]).astype(BF16)
                km = (kk[h] * far[h]).astype(BF16)
                sc = lax.dot_general(qm, km, NT_DIMS, preferred_element_type=F32)
                scores[h] = jnp.where(pair_level == level, sc, scores[h])
                tot_before = pltpu.roll(tot[h], m, 0)
                tot_after = pltpu.roll(tot[h], C - m, 0)
                if reverse:
                    near[h] = jnp.where(bit, near[h], near[h] * tot_after)
                    far[h] = jnp.where(bit, far[h] * tot_before, far[h])
                else:
                    near[h] = jnp.where(bit, near[h] * tot_before, near[h])
                    far[h] = jnp.where(bit, far[h], far[h] * tot_after)
                tot[h] = tot[h] * jnp.where(bit, tot_before, tot_after)
            m *= 2
            level += 1

        for h in range(heads):
            lanes = slice(h * HEAD_DIM, (h + 1) * HEAD_DIM)
            vb = v[h].astype(BF16)
            st = st_ref[h]
            o = jnp.dot(scores[h].astype(BF16), vb, preferred_element_type=F32)
            o = o + jnp.sum(q[h] * kk[h], axis=1, keepdims=True) * v[h]
            o = o + lax.dot_general((q[h] * near[h]).astype(BF16), st.astype(BF16), NT_DIMS,
                                    preferred_element_type=F32)
            st_ref[h] = st * tot[h][0:1] + jnp.dot(v[h].T.astype(BF16), (kk[h] * far[h]).astype(BF16),
                                                   preferred_element_type=F32)
            if finalize:
                o = o + prev_ref[0, rows, lanes]
                ms = jnp.mean(o * o, axis=-1, keepdims=True)
                o = o * lax.rsqrt(ms + EPS) * gn_ref[...]
            o_ref[0, rows, lanes] = o.astype(o_ref.dtype)
        return carry

    lax.fori_loop(0, n, one_chunk, 0)


def hgrn_direction(u, lb, gate_block, *, reverse, prev=None, o_norm=None, rows=1024, chunk=128, heads=4):
    B, S, N = u.shape
    E = N // 5
    W = heads * HEAD_DIM
    HB = E // W
    R = min(rows, S)
    nb = S // R
    finalize = prev is not None
    ridx = (lambda c: nb - 1 - c) if reverse else (lambda c: c)
    col = lambda blk: pl.BlockSpec((1, R, W), lambda b, h, c: (b, ridx(c), blk * HB + h))
    in_specs = [col(0), col(gate_block), col(3), pl.BlockSpec((1, W), lambda b, h, c: (0, h))]
    args = [u, u, u, lb.reshape(1, E)]
    if finalize:
        in_specs += [pl.BlockSpec((1, R, W), lambda b, h, c: (b, ridx(c), h)),
                     pl.BlockSpec((1, HEAD_DIM), lambda b, h, c: (0, 0))]
        args += [prev, o_norm.reshape(1, HEAD_DIM)]
    return pl.pallas_call(
        functools.partial(_hgrn_kernel, reverse=reverse, finalize=finalize, chunk=min(chunk, R)),
        grid=(B, HB, nb),
        in_specs=in_specs,
        out_specs=pl.BlockSpec((1, R, W), lambda b, h, c: (b, ridx(c), h)),
        out_shape=jax.ShapeDtypeStruct((B, S, E), BF16 if finalize else F32),
        scratch_shapes=[pltpu.VMEM((heads, HEAD_DIM, HEAD_DIM), F32)],
        compiler_params=_params(("parallel", "parallel", "arbitrary"), 40),
        name="hgrn_bwd" if reverse else "hgrn_fwd",
    )(*args)


def _attn_prep_kernel(q_ref, k_ref, v_ref, cf_ref, sa_ref, sb_ref, qn_ref, kn_ref, qt_ref, ko_ref, vt_ref,
                      *, q_scale):
    cf, sa, sb = cf_ref[0], sa_ref[0], sb_ref[0]
    half = ROPE_DIM // 2

    def norm_rope(x, gain):
        ms = jnp.mean(x * x, axis=-1, keepdims=True)
        xn = x * lax.rsqrt(ms + EPS) * gain
        return xn * cf + pltpu.roll(xn, LANES - half, 1) * sa + pltpu.roll(xn, half, 1) * sb

    for g in range(q_ref.shape[2] // HEAD_DIM):
        sl = slice(g * HEAD_DIM, (g + 1) * HEAD_DIM)
        qt_ref[0, sl, :] = (norm_rope(q_ref[0, :, sl], qn_ref[...]) * q_scale).T.astype(BF16)
        ko_ref[0, :, sl] = norm_rope(k_ref[0, :, sl], kn_ref[...]).astype(BF16)
        vt_ref[0, sl, :] = v_ref[0, :, sl].T.astype(BF16)


def attn_prep(u, rope_cf, rope_sa, rope_sb, q_norm, k_norm, *, tm=256):
    B, S, N = u.shape
    E = N // 4
    tm = min(tm, S)
    col = lambda blk: pl.BlockSpec((1, tm, E), lambda b, i: (b, i, blk))
    tab = pl.BlockSpec((1, tm, LANES), lambda b, i: (b, i, 0))
    gain = pl.BlockSpec((1, HEAD_DIM), lambda b, i: (0, 0))
    rows = pl.BlockSpec((1, tm, E), lambda b, i: (b, i, 0))
    cols = pl.BlockSpec((1, E, tm), lambda b, i: (b, 0, i))
    tables = [t.reshape(B, S, LANES) for t in (rope_cf, rope_sa, rope_sb)]
    return pl.pallas_call(
        functools.partial(_attn_prep_kernel, q_scale=math.log2(math.e) / math.sqrt(HEAD_DIM)),
        grid=(B, S // tm),
        in_specs=[col(0), col(1), col(2), tab, tab, tab, gain, gain],
        out_specs=[cols, rows, cols],
        out_shape=[jax.ShapeDtypeStruct((B, E, S), BF16), jax.ShapeDtypeStruct((B, S, E), BF16),
                   jax.ShapeDtypeStruct((B, E, S), BF16)],
        compiler_params=_params(("parallel", "parallel"), 48),
        name="attn_prep",
    )(u, u, u, *tables, q_norm.reshape(1, HEAD_DIM), k_norm.reshape(1, HEAD_DIM))


def _attn_kernel(lam_ref, qt_ref, k_ref, vt_ref, sn_ref, o_ref, m_ref, l_ref, acc_ref, s00, s01, s10, s11, mc_ref,
                 *, post_scale, tkc):
    n = k_ref.shape[1] // tkc
    s_refs = ((s00, s01), (s10, s11))
    m_ref[...] = jnp.full_like(m_ref, -jnp.inf)
    l_ref[...] = jnp.zeros_like(l_ref)
    acc_ref[...] = jnp.zeros_like(acc_ref)

    def produce(c, slot, i):
        sl = slice(i * HEAD_DIM, (i + 1) * HEAD_DIM)
        off = pl.multiple_of(c * tkc, tkc)
        s_new = jnp.dot(k_ref[0, pl.ds(off, tkc), sl], qt_ref[0, sl, :], preferred_element_type=F32)
        s_refs[slot][i][...] = s_new
        mc_ref[slot, i] = jnp.max(s_new, axis=0, keepdims=True)

    def consume(vt_c, slot, i):
        m_prev = m_ref[i]
        m_new = jnp.maximum(m_prev, mc_ref[slot, i])
        alpha = jnp.exp2(m_prev - m_new)
        p = jnp.exp2(s_refs[slot][i][...] - m_new)
        l_ref[i] = alpha * l_ref[i] + jnp.sum(p, axis=0, keepdims=True)
        acc_ref[i] = alpha * acc_ref[i] + jnp.dot(vt_c, p.astype(BF16), preferred_element_type=F32)
        m_ref[i] = m_new

    def step(c, slot, make_next=True):
        vt_c = vt_ref[0, :, pl.ds(pl.multiple_of(c * tkc, tkc), tkc)]
        for i in range(2):
            if make_next:
                produce(c + 1, 1 - slot, i)
            consume(vt_c, slot, i)

    produce(0, 0, 0)
    produce(0, 0, 1)

    def pair(j, carry):
        step(2 * j, 0)
        step(2 * j + 1, 1)
        return carry

    lax.fori_loop(0, n // 2 - 1, pair, 0)
    step(n - 2, 0)
    step(n - 1, 1, make_next=False)
    o = (acc_ref[0] / l_ref[0] - lam_ref[0] * (acc_ref[1] / l_ref[1])).T
    ms = jnp.mean(o * o, axis=-1, keepdims=True)
    o_ref[0] = (o * lax.rsqrt(ms + EPS) * sn_ref[...] * post_scale).astype(o_ref.dtype)


def diff_attention_core(qt, kp, vt, lam, sub_norm, post_scale, *, tq=512, tkc=1024):
    B, S, E = kp.shape
    W = 2 * HEAD_DIM
    H = E // W
    tq, tkc = min(tq, S), min(tkc, S // 2)
    assert S % (2 * tkc) == 0
    return pl.pallas_call(
        functools.partial(_attn_kernel, post_scale=post_scale, tkc=tkc),
        grid=(B, H, S // tq),
        in_specs=[
            pl.BlockSpec(memory_space=pltpu.SMEM),
            pl.BlockSpec((1, W, tq), lambda b, h, i: (b, h, i)),
            pl.BlockSpec((1, S, W), lambda b, h, i: (b, 0, h)),
            pl.BlockSpec((1, W, S), lambda b, h, i: (b, h, 0)),
            pl.BlockSpec((1, W), lambda b, h, i: (0, 0)),
        ],
        out_specs=pl.BlockSpec((1, tq, W), lambda b, h, i: (b, i, h)),
        out_shape=jax.ShapeDtypeStruct((B, S, E), BF16),
        scratch_shapes=[pltpu.VMEM((2, 1, tq), F32), pltpu.VMEM((2, 1, tq), F32), pltpu.VMEM((2, W, tq), F32)]
        + [pltpu.VMEM((tkc, tq), F32)] * 4 + [pltpu.VMEM((2, 2, 1, tq), F32)],
        compiler_params=_params(("parallel", "parallel", "arbitrary"), 56),
        name="diff_attention",
    )(lam.reshape(1), qt, kp, vt, sub_norm.reshape(1, W))


def _dft_tables(n):
    ang = 2.0 * np.pi * ((np.arange(n)[:, None] * np.arange(n)[None, :]) % n) / n
    return jnp.asarray(np.cos(ang), F32), jnp.asarray(np.sin(ang), F32)


def _fnet_a_kernel(x_ref, c_ref, s_ref, tc_ref, ts_ref, ar_ref, ai_ref):
    c, s = c_ref[...], s_ref[...]
    for j in range(x_ref.shape[2]):
        x = x_ref[0, :, j, :].astype(BF16)
        ar = jnp.dot(c, x, preferred_element_type=F32)
        ai = -jnp.dot(s, x, preferred_element_type=F32)
        tc, ts = tc_ref[j], ts_ref[j]
        ar_ref[0, j] = ar * tc + ai * ts
        ai_ref[0, j] = ai * tc - ar * ts


def _fnet_b_kernel(ar_ref, ai_ref, c_ref, s_ref, cc_ref, sc_ref, gw_ref, y_ref, *, norm):
    c, s = c_ref[...], s_ref[...]
    n_lo, digits = ar_ref.shape[1], ar_ref.shape[2]
    G = gw_ref.shape[2]
    br, bi = [], []
    for j in range(digits):
        ar, ai = ar_ref[0, :, j, :].astype(BF16), ai_ref[0, :, j, :].astype(BF16)
        br.append((jnp.dot(c, ar, preferred_element_type=F32)
                   + jnp.dot(s, ai, preferred_element_type=F32)).astype(BF16))
        bi.append((jnp.dot(c, ai, preferred_element_type=F32)
                   - jnp.dot(s, ar, preferred_element_type=F32)).astype(BF16))
    br, bi = jnp.concatenate(br, axis=0), jnp.concatenate(bi, axis=0)
    for g in range(gw_ref.shape[0]):
        sl = slice(g * G, (g + 1) * G)
        f = (jnp.dot(br[:, sl], cc_ref[...], preferred_element_type=F32)
             + jnp.dot(bi[:, sl], sc_ref[...], preferred_element_type=F32)) * norm
        yg = jnp.dot(f.astype(BF16), gw_ref[g], preferred_element_type=F32)
        for j in range(digits):
            y_ref[0, :, j, sl] = yg[j * n_lo:(j + 1) * n_lo]


def fourier_core(u, group_w, *, n_lo=128, digits=8, e_chunk=512):
    B, S, N = u.shape
    E = N // 2
    G = E // FN_GROUPS
    n_hi = S // n_lo
    Ec = min(e_chunk, E)
    c_hi, s_hi = _dft_tables(n_hi)
    c_lo, s_lo = _dft_tables(n_lo)
    c_ch, s_ch = _dft_tables(G)
    ang = 2.0 * np.pi * ((np.arange(n_lo)[:, None] * np.arange(n_hi)[None, :]) % S) / S
    tw_c = jnp.asarray(np.cos(ang), F32)[:, :, None]
    tw_s = jnp.asarray(np.sin(ang), F32)[:, :, None]
    mat = lambda n: pl.BlockSpec((n, n), lambda b, j, e: (0, 0))
    tw = pl.BlockSpec((digits, n_hi, 1), lambda b, j, e: (j, 0, 0))
    a_out = pl.BlockSpec((1, digits, n_hi, Ec), lambda b, j, e: (b, j, 0, e))
    ar, ai = pl.pallas_call(
        _fnet_a_kernel,
        grid=(B, n_lo // digits, E // Ec),
        in_specs=[pl.BlockSpec((1, n_hi, digits, Ec), lambda b, j, e: (b, 0, j, e)), mat(n_hi), mat(n_hi), tw, tw],
        out_specs=[a_out, a_out],
        out_shape=[jax.ShapeDtypeStruct((B, n_lo, n_hi, E), F32)] * 2,
        compiler_params=_params(("parallel", "parallel", "parallel"), 32),
        name="fnet_stage_a",
    )(u.reshape(B, n_hi, n_lo, N), c_hi.astype(BF16), s_hi.astype(BF16), tw_c, tw_s)

    kd = min(digits, n_hi)
    a_in = pl.BlockSpec((1, n_lo, kd, Ec), lambda b, j, e: (b, 0, j, e))
    y = pl.pallas_call(
        functools.partial(_fnet_b_kernel, norm=1.0 / math.sqrt(S * G)),
        grid=(B, n_hi // kd, E // Ec),
        in_specs=[a_in, a_in, mat(n_lo), mat(n_lo), mat(G), mat(G),
                  pl.BlockSpec((Ec // G, G, G), lambda b, j, e: (e, 0, 0))],
        out_specs=pl.BlockSpec((1, n_lo, kd, Ec), lambda b, j, e: (b, 0, j, e)),
        out_shape=jax.ShapeDtypeStruct((B, n_lo, n_hi, E), F32),
        compiler_params=_params(("parallel", "parallel", "parallel"), 32),
        name="fnet_stage_b",
    )(ar, ai, c_lo.astype(BF16), s_lo.astype(BF16), c_ch.astype(BF16), s_ch.astype(BF16), group_w.astype(BF16))
    return y.reshape(B, S, E)


def conv_layer(x, norm, w_in, dw, dw_b, ln_g, ln_b, w_out):
    B, S, D = x.shape
    E = w_out.shape[0]
    u = norm_matmul(x.reshape(B * S, D), norm, w_in)
    o = conv_mixer(u.reshape(B, S, 3 * E), dw, dw_b, ln_g, ln_b)
    return gate_out(o.reshape(B * S, E), u, 2, x.reshape(B * S, D), w_out).reshape(B, S, D)


def hgrn_layer(x, norm, w_in, lb_fwd, lb_bwd, o_norm, w_out):
    B, S, D = x.shape
    E = w_out.shape[0]
    u = norm_matmul(x.reshape(B * S, D), norm, w_in)
    u3 = u.reshape(B, S, 5 * E)
    o_fwd = hgrn_direction(u3, lb_fwd, 1, reverse=False)
    o = hgrn_direction(u3, lb_bwd, 2, reverse=True, prev=o_fwd, o_norm=o_norm)
    return gate_out(o.reshape(B * S, E), u, 4, x.reshape(B * S, D), w_out).reshape(B, S, D)


def rope_lane_tables(positions):
    half = ROPE_DIM // 2
    inv = 1.0 / (ROPE_THETA ** (jnp.arange(0, ROPE_DIM, 2, dtype=F32) / ROPE_DIM))
    ang = positions.astype(F32).reshape(-1, 1) * inv
    cos, sin = jnp.cos(ang), jnp.sin(ang)
    T = ang.shape[0]
    rest = LANES - ROPE_DIM
    cf = jnp.concatenate([cos, cos, jnp.ones((T, rest), F32)], axis=-1)
    sa = jnp.concatenate([-sin, jnp.zeros((T, LANES - half), F32)], axis=-1)
    sb = jnp.concatenate([jnp.zeros((T, half), F32), sin, jnp.zeros((T, rest), F32)], axis=-1)
    return cf, sa, sb


def diff_layer(x, rope, norm, w_in, q_norm, k_norm, lam_q1, lam_k1, lam_q2, lam_k2, sub_norm, w_out, lam_init):
    B, S, D = x.shape
    E = w_out.shape[0]
    u = norm_matmul(x.reshape(B * S, D), norm, w_in)
    qt, kp, vt = attn_prep(u.reshape(B, S, 4 * E), *rope, q_norm, k_norm)
    lam = jnp.exp(jnp.sum(lam_q1 * lam_k1)) - jnp.exp(jnp.sum(lam_q2 * lam_k2)) + lam_init
    o = diff_attention_core(qt, kp, vt, lam, sub_norm, 1.0 - lam_init)
    return gate_out(o.reshape(B * S, E), u, 3, x.reshape(B * S, D), w_out).reshape(B, S, D)


def fnet_layer(x, norm, w_in, group_w, w_out):
    B, S, D = x.shape
    E = w_out.shape[0]
    u = norm_matmul(x.reshape(B * S, D), norm, w_in)
    y = fourier_core(u.reshape(B, S, 2 * E), group_w)
    return gate_out(y.reshape(B * S, E), u, 1, x.reshape(B * S, D), w_out).reshape(B, S, D)


def hgrn_lower_bounds(table):
    lb = jnp.cumsum(jax.nn.softmax(table.astype(F32), axis=0), axis=0)
    return lb - lb[0:1]


def kernel(x, positions, conv_norm, conv_w_in, conv_dw, conv_dw_b, conv_ln_g, conv_ln_b, conv_w_out, hgrn_norm, hgrn_w_in, hgrn_lb_fwd, hgrn_lb_bwd, hgrn_o_norm, hgrn_w_out, diff_norm, diff_w_in, diff_q_norm, diff_k_norm, diff_lam_q1, diff_lam_k1, diff_lam_q2, diff_lam_k2, diff_sub_norm, diff_w_out, fnet_norm, fnet_w_in, fnet_group_w, fnet_w_out):
    depth = hgrn_lb_fwd.shape[0]
    n_mixers = 4
    rope = rope_lane_tables(positions)
    lb_fwd = hgrn_lower_bounds(hgrn_lb_fwd)
    lb_bwd = hgrn_lower_bounds(hgrn_lb_bwd)
    for layer in range(depth):
        m, j = layer % n_mixers, layer // n_mixers
        if m == 0:
            x = conv_layer(x, conv_norm[j], conv_w_in[j], conv_dw[j], conv_dw_b[j], conv_ln_g[j], conv_ln_b[j],
                           conv_w_out[j])
        elif m == 1:
            x = hgrn_layer(x, hgrn_norm[j], hgrn_w_in[j], lb_fwd[layer], lb_bwd[layer], hgrn_o_norm[j],
                           hgrn_w_out[j])
        elif m == 2:
            lam_init = 0.8 - 0.6 * math.exp(-0.3 * layer)
            x = diff_layer(x, rope, diff_norm[j], diff_w_in[j], diff_q_norm[j], diff_k_norm[j], diff_lam_q1[j],
                           diff_lam_k1[j], diff_lam_q2[j], diff_lam_k2[j], diff_sub_norm[j], diff_w_out[j], lam_init)
        else:
            x = fnet_layer(x, fnet_norm[j], fnet_w_in[j], fnet_group_w[j], fnet_w_out[j])
    return x
```

```python
import functools
import math

import numpy as np
import jax
import jax.numpy as jnp
from jax import lax
from jax.experimental import pallas as pl
from jax.experimental.pallas import tpu as pltpu

EPS = 1e-6
LANES = 128
SUBLANES = 8
HEAD_DIM = 128
CONV_HALO = 16
ROPE_DIM = 32
ROPE_THETA = 500000.0
FN_GROUPS = 8
MIB = 1 << 20

F32 = jnp.float32
BF16 = jnp.bfloat16
NT_DIMS = (((1,), (1,)), ((), ()))


def _params(semantics, vmem_mib):
    return pltpu.CompilerParams(dimension_semantics=semantics, vmem_limit_bytes=vmem_mib * MIB)


def _silu(z):
    return z * jax.nn.sigmoid(z)


def _norm_matmul_kernel(x_ref, g_ref, w_ref, o_ref, h_ref):
    @pl.when(pl.program_id(1) == 0)
    def _():
        x = x_ref[...]
        ms = jnp.mean(x * x, axis=-1, keepdims=True)
        h_ref[...] = (x * lax.rsqrt(ms + EPS) * g_ref[...]).astype(h_ref.dtype)

    o_ref[...] = jnp.dot(h_ref[...], w_ref[...], preferred_element_type=F32).astype(o_ref.dtype)


def norm_matmul(x2d, gain, w, *, tm=1024, tn=1024):
    T, D = x2d.shape
    N = w.shape[1]
    tm, tn = min(tm, T), min(tn, N)
    return pl.pallas_call(
        _norm_matmul_kernel,
        grid=(T // tm, N // tn),
        in_specs=[
            pl.BlockSpec((tm, D), lambda i, j: (i, 0)),
            pl.BlockSpec((1, D), lambda i, j: (0, 0)),
            pl.BlockSpec((D, tn), lambda i, j: (0, j)),
        ],
        out_specs=pl.BlockSpec((tm, tn), lambda i, j: (i, j)),
        out_shape=jax.ShapeDtypeStruct((T, N), F32),
        scratch_shapes=[pltpu.VMEM((tm, D), BF16)],
        compiler_params=_params(("parallel", "arbitrary"), 48),
        name="norm_matmul",
    )(x2d, gain.reshape(1, D), w.astype(BF16))


def _gate_out_kernel(o_ref, z_ref, x_ref, w_ref, out_ref):
    g = (o_ref[...] * _silu(z_ref[...])).astype(BF16)
    out_ref[...] = x_ref[...] + jnp.dot(g, w_ref[...], preferred_element_type=F32)


def gate_out(o2d, u2d, z_block, x2d, w_out, *, tm=256):
    T, E = o2d.shape
    D = w_out.shape[1]
    tm = min(tm, T)
    return pl.pallas_call(
        _gate_out_kernel,
        grid=(T // tm,),
        in_specs=[
            pl.BlockSpec((tm, E), lambda i: (i, 0)),
            pl.BlockSpec((tm, E), lambda i: (i, z_block)),
            pl.BlockSpec((tm, D), lambda i: (i, 0)),
            pl.BlockSpec((E, D), lambda i: (0, 0)),
        ],
        out_specs=pl.BlockSpec((tm, D), lambda i: (i, 0)),
        out_shape=jax.ShapeDtypeStruct((T, D), F32),
        compiler_params=_params(("parallel",), 48),
        name="gate_out",
    )(o2d, u2d, x2d, w_out.astype(BF16))


def _conv_kernel(am_ref, ap_ref, an_ref, bm_ref, bp_ref, bn_ref, dw_ref, dwb_ref, g_ref, beta_ref,
                 o_ref, vbuf, shifted, cbuf, *, taps, row_block, col_block):
    s = pl.program_id(1)
    ts, E = am_ref.shape[1], am_ref.shape[2]
    glu = lambda a, b: a * jax.nn.sigmoid(b)
    vbuf[0:CONV_HALO, :] = jnp.where(s > 0, glu(ap_ref[0], bp_ref[0]), 0.0)
    vbuf[CONV_HALO:CONV_HALO + ts, :] = glu(am_ref[0], bm_ref[0])
    vbuf[CONV_HALO + ts:, :] = jnp.where(s < pl.num_programs(1) - 1, glu(an_ref[0], bn_ref[0]), 0.0)
    span = shifted.shape[1]
    for r in range(1, SUBLANES):
        shifted[r - 1] = vbuf[r:r + span, :]
    first = CONV_HALO - taps // 2
    for r0 in range(0, ts, row_block):
        for c0 in range(0, E, col_block):
            cols = slice(c0, c0 + col_block)
            acc = jnp.zeros((row_block, col_block), F32)
            for k in range(taps):
                phase, base = (first + k) % SUBLANES, r0 + (first + k) // SUBLANES * SUBLANES
                win = (vbuf[base:base + row_block, cols] if phase == 0
                       else shifted[phase - 1, base:base + row_block, cols])
                acc = acc + dw_ref[k:k + 1, cols] * win
            cbuf[r0:r0 + row_block, cols] = acc + dwb_ref[:, cols]
    c = cbuf[...]
    mu = jnp.mean(c, axis=-1, keepdims=True)
    cc = c - mu
    var = jnp.mean(cc * cc, axis=-1, keepdims=True)
    y = cc * lax.rsqrt(var + EPS) * g_ref[...] + beta_ref[...]
    o_ref[0] = _silu(y).astype(o_ref.dtype)


def conv_mixer(u, dw, dw_b, ln_g, ln_b, *, ts=256):
    B, S, N = u.shape
    E = N // 3
    taps = dw.shape[0]
    ts = min(ts, S)
    hb = ts // CONV_HALO
    last_halo = S // CONV_HALO - 1
    main = lambda col: pl.BlockSpec((1, ts, E), lambda b, s: (b, s, col))
    prev = lambda col: pl.BlockSpec((1, CONV_HALO, E), lambda b, s: (b, jnp.maximum(s * hb - 1, 0), col))
    nxt = lambda col: pl.BlockSpec((1, CONV_HALO, E), lambda b, s: (b, jnp.minimum((s + 1) * hb, last_halo), col))
    vec = lambda rows: pl.BlockSpec((rows, E), lambda b, s: (0, 0))
    return pl.pallas_call(
        functools.partial(_conv_kernel, taps=taps, row_block=32, col_block=512),
        grid=(B, S // ts),
        in_specs=[main(0), prev(0), nxt(0), main(1), prev(1), nxt(1), vec(taps), vec(1), vec(1), vec(1)],
        out_specs=pl.BlockSpec((1, ts, E), lambda b, s: (b, s, 0)),
        out_shape=jax.ShapeDtypeStruct((B, S, E), BF16),
        scratch_shapes=[pltpu.VMEM((ts + 2 * CONV_HALO, E), F32),
                        pltpu.VMEM((SUBLANES - 1, ts + 2 * CONV_HALO - SUBLANES, E), F32),
                        pltpu.VMEM((ts, E), F32)],
        compiler_params=_params(("parallel", "parallel"), 56),
        name="conv_mixer",
    )(u, u, u, u, u, u, dw, dw_b.reshape(1, E), ln_g.reshape(1, E), ln_b.reshape(1, E))


def _hgrn_kernel(*refs, reverse, finalize, chunk):
    if finalize:
        q_ref, a_ref, v_ref, lb_ref, prev_ref, gn_ref, o_ref, st_ref = refs
    else:
        q_ref, a_ref, v_ref, lb_ref, o_ref, st_ref = refs
    C = chunk
    n = q_ref.shape[1] // C
    heads = q_ref.shape[2] // HEAD_DIM

    @pl.when(pl.program_id(2) == 0)
    def _():
        st_ref[...] = jnp.zeros_like(st_ref)

    row = lax.broadcasted_iota(jnp.int32, (C, HEAD_DIM), 0)
    pair_xor = lax.broadcasted_iota(jnp.int32, (C, C), 0) ^ lax.broadcasted_iota(jnp.int32, (C, C), 1)
    tt, ss = lax.broadcasted_iota(jnp.int32, (C, C), 0), lax.broadcasted_iota(jnp.int32, (C, C), 1)
    pair_level = jnp.where((tt < ss) if reverse else (tt > ss),
                           (pltpu.bitcast(pair_xor.astype(F32), jnp.int32) >> 23) - 127, -1)

    def one_chunk(j, carry):
        r0 = pl.multiple_of((n - 1 - j if reverse else j) * C, C)
        rows = pl.ds(r0, C)
        q, kk, v, near, far, tot, scores = [], [], [], [], [], [], []
        for h in range(heads):
            lanes = slice(h * HEAD_DIM, (h + 1) * HEAD_DIM)
            a, lb = a_ref[0, rows, lanes], lb_ref[:, lanes]
            e = jnp.exp(-jnp.abs(a))
            r = 1.0 / (1.0 + e)
            er = e * r
            pos = a >= 0
            f = lb + (1.0 - lb) * jnp.where(pos, r, er)
            q.append(q_ref[0, rows, lanes])
            v.append(v_ref[0, rows, lanes])
            kk.append((1.0 - lb) * jnp.where(pos, er, r))
            near.append(f)
            far.append(jnp.ones_like(f))
            tot.append(f)
            scores.append(jnp.zeros((C, C), F32))

        m, level = 1, 0
        while m < C:
            bit = (row & m) != 0
            for h in range(heads):
                qm = (q[h] * near[h]).astype(BF16)
                km = (kk[h] * far[h]).astype(BF16)
                sc = lax.dot_general(qm, km, NT_DIMS, preferred_element_type=F32)
                scores[h] = jnp.where(pair_level == level, sc, scores[h])
                tot_before = pltpu.roll(tot[h], m, 0)
                tot_after = pltpu.roll(tot[h], C - m, 0)
                if reverse:
                    near[h] = jnp.where(bit, near[h], near[h] * tot_after)
                    far[h] = jnp.where(bit, far[h] * tot_before, far[h])
                else:
                    near[h] = jnp.where(bit, near[h] * tot_before, near[h])
                    far[h] = jnp.where(bit, far[h], far[h] * tot_after)
                tot[h] = tot[h] * jnp.where(bit, tot_before, tot_after)
            m *= 2
            level += 1

        for h in range(heads):
            lanes = slice(h * HEAD_DIM, (h + 1) * HEAD_DIM)
            vb = v[h].astype(BF16)
            st = st_ref[h]
            o = jnp.dot(scores[h].astype(BF16), vb, preferred_element_type=F32)
            o = o + jnp.sum(q[h] * kk[h], axis=1, keepdims=True) * v[h]
            o = o + lax.dot_general((q[h] * near[h]).astype(BF16), st.astype(BF16), NT_DIMS,
                                    preferred_element_type=F32)
            st_ref[h] = st * tot[h][0:1] + jnp.dot(v[h].T.astype(BF16), (kk[h] * far[h]).astype(BF16),
                                                   preferred_element_type=F32)
            if finalize:
                o = o + prev_ref[0, rows, lanes]
                ms = jnp.mean(o * o, axis=-1, keepdims=True)
                o = o * lax.rsqrt(ms + EPS) * gn_ref[...]
            o_ref[0, rows, lanes] = o.astype(o_ref.dtype)
        return carry

    lax.fori_loop(0, n, one_chunk, 0)


def hgrn_direction(u, lb, gate_block, *, reverse, prev=None, o_norm=None, rows=1024, chunk=128, heads=4):
    B, S, N = u.shape
    E = N // 5
    W = heads * HEAD_DIM
    HB = E // W
    R = min(rows, S)
    nb = S // R
    finalize = prev is not None
    ridx = (lambda c: nb - 1 - c) if reverse else (lambda c: c)
    col = lambda blk: pl.BlockSpec((1, R, W), lambda b, h, c: (b, ridx(c), blk * HB + h))
    in_specs = [col(0), col(gate_block), col(3), pl.BlockSpec((1, W), lambda b, h, c: (0, h))]
    args = [u, u, u, lb.reshape(1, E)]
    if finalize:
        in_specs += [pl.BlockSpec((1, R, W), lambda b, h, c: (b, ridx(c), h)),
                     pl.BlockSpec((1, HEAD_DIM), lambda b, h, c: (0, 0))]
        args += [prev, o_norm.reshape(1, HEAD_DIM)]
    return pl.pallas_call(
        functools.partial(_hgrn_kernel, reverse=reverse, finalize=finalize, chunk=min(chunk, R)),
        grid=(B, HB, nb),
        in_specs=in_specs,
        out_specs=pl.BlockSpec((1, R, W), lambda b, h, c: (b, ridx(c), h)),
        out_shape=jax.ShapeDtypeStruct((B, S, E), BF16 if finalize else F32),
        scratch_shapes=[pltpu.VMEM((heads, HEAD_DIM, HEAD_DIM), F32)],
        compiler_params=_params(("parallel", "parallel", "arbitrary"), 40),
        name="hgrn_bwd" if reverse else "hgrn_fwd",
    )(*args)


def _attn_proj_kernel(x_ref, g_ref, w_ref, *rest, mode, scale, slab):
    if mode == "v":
        o_ref, h_ref = rest
    else:
        t1_ref, t2_ref, gn_ref, *rest = rest
        if mode == "k":
            swap_ref, o_ref, h_ref = rest
        else:
            o_ref, h_ref = rest

    @pl.when(pl.program_id(1) == 0)
    def _():
        x = x_ref[...]
        ms = jnp.mean(x * x, axis=-1, keepdims=True)
        h_ref[...] = (x * lax.rsqrt(ms + EPS) * g_ref[...]).astype(h_ref.dtype)

    half = ROPE_DIM // 2
    project = lambda c0: jnp.dot(h_ref[...], w_ref[:, c0:c0 + slab], preferred_element_type=F32)
    tn = w_ref.shape[1]
    nxt = project(0)
    for c0 in range(0, tn, slab):
        acc, nxt = nxt, (project(c0 + slab) if c0 + slab < tn else None)
        for g0 in range(0, slab, HEAD_DIM):
            y = acc[:, g0:g0 + HEAD_DIM]
            cols = slice(c0 + g0, c0 + g0 + HEAD_DIM)
            if mode == "q":
                yt = y.T
                yt = yt * lax.rsqrt(jnp.mean(yt * yt, axis=0, keepdims=True) + EPS) * (gn_ref[...] * scale)
                lo, hi, cos, sin = yt[:half], yt[half:ROPE_DIM], t1_ref[...], t2_ref[...]
                yt = jnp.concatenate([lo * cos - hi * sin, hi * cos + lo * sin, yt[ROPE_DIM:]], axis=0)
                o_ref[0, cols, :] = yt.astype(o_ref.dtype)
            elif mode == "k":
                yn = y * lax.rsqrt(jnp.mean(y * y, axis=-1, keepdims=True) + EPS) * gn_ref[...]
                partner = jnp.dot(yn.astype(BF16), swap_ref[...], preferred_element_type=F32)
                o_ref[:, cols] = (yn * t1_ref[...] + partner * t2_ref[...]).astype(o_ref.dtype)
            else:
                o_ref[0, cols, :] = y.T.astype(o_ref.dtype)


def attn_projection(x, gain, w, mode, tables=None, head_gain=None, *, tm=1024, tn=1024):
    B, S, D = x.shape
    E = w.shape[1]
    tm, tn = min(tm, S), min(tn, E)
    sb = S // tm
    if mode == "k":
        out_spec = pl.BlockSpec((tm, tn), lambda i, j: (i, j))
        out_shape = jax.ShapeDtypeStruct((B * S, E), BF16)
    else:
        out_spec = pl.BlockSpec((1, tn, tm), lambda i, j: (i // sb, j, i % sb))
        out_shape = jax.ShapeDtypeStruct((B, E, S), BF16)
    in_specs = [
        pl.BlockSpec((tm, D), lambda i, j: (i, 0)),
        pl.BlockSpec((1, D), lambda i, j: (0, 0)),
        pl.BlockSpec((D, tn), lambda i, j: (0, j)),
    ]
    args = [x.reshape(B * S, D), gain.reshape(1, D), w.astype(BF16)]
    half = ROPE_DIM // 2
    if mode == "q":
        in_specs += [pl.BlockSpec((half, tm), lambda i, j: (0, i))] * 2 + [pl.BlockSpec((HEAD_DIM, 1), lambda i, j: (0, 0))]
        args += [*tables, head_gain.reshape(HEAD_DIM, 1)]
    elif mode == "k":
        swap = np.zeros((HEAD_DIM, HEAD_DIM), np.float32)
        swap[np.arange(half) + half, np.arange(half)] = 1.0
        swap[np.arange(half), np.arange(half) + half] = 1.0
        in_specs += [pl.BlockSpec((tm, LANES), lambda i, j: (i, 0))] * 2 + [
            pl.BlockSpec((1, HEAD_DIM), lambda i, j: (0, 0)), pl.BlockSpec((HEAD_DIM, HEAD_DIM), lambda i, j: (0, 0))]
        args += [*tables, head_gain.reshape(1, HEAD_DIM), jnp.asarray(swap, BF16)]
    out = pl.pallas_call(
        functools.partial(_attn_proj_kernel, mode=mode, scale=math.log2(math.e) / math.sqrt(HEAD_DIM),
                          slab=min(2 * HEAD_DIM, tn)),
        grid=(B * S // tm, E // tn),
        in_specs=in_specs,
        out_specs=out_spec,
        out_shape=out_shape,
        scratch_shapes=[pltpu.VMEM((tm, D), BF16)],
        compiler_params=_params(("parallel", "arbitrary"), 48),
        name="attn_proj_" + mode,
    )(*args)
    return out.reshape(B, S, E) if mode == "k" else out


def _attn_kernel(lam_ref, qt_ref, k_ref, vt_ref, sn_ref, o_ref, m_ref, l_ref, acc_ref, s00, s01, s10, s11, mc_ref,
                 *, post_scale, tkc):
    n = k_ref.shape[1] // tkc
    s_refs = ((s00, s01), (s10, s11))
    m_ref[...] = jnp.full_like(m_ref, -jnp.inf)
    l_ref[...] = jnp.zeros_like(l_ref)
    acc_ref[...] = jnp.zeros_like(acc_ref)

    def produce(c, slot, i):
        sl = slice(i * HEAD_DIM, (i + 1) * HEAD_DIM)
        off = pl.multiple_of(c * tkc, tkc)
        s_new = jnp.dot(k_ref[0, pl.ds(off, tkc), sl], qt_ref[0, sl, :], preferred_element_type=F32)
        s_refs[slot][i][...] = s_new
        mc_ref[slot, i] = jnp.max(s_new, axis=0, keepdims=True)

    def consume(vt_c, slot, i):
        m_prev = m_ref[i]
        m_new = jnp.maximum(m_prev, mc_ref[slot, i])
        alpha = jnp.exp2(m_prev - m_new)
        p = jnp.exp2(s_refs[slot][i][...] - m_new)
        l_ref[i] = alpha * l_ref[i] + jnp.sum(p, axis=0, keepdims=True)
        acc_ref[i] = alpha * acc_ref[i] + jnp.dot(vt_c, p.astype(BF16), preferred_element_type=F32)
        m_ref[i] = m_new

    def step(c, slot, make_next=True):
        vt_c = vt_ref[0, :, pl.ds(pl.multiple_of(c * tkc, tkc), tkc)]
        for i in range(2):
            if make_next:
                produce(c + 1, 1 - slot, i)
            consume(vt_c, slot, i)

    produce(0, 0, 0)
    produce(0, 0, 1)

    def pair(j, carry):
        step(2 * j, 0)
        step(2 * j + 1, 1)
        return carry

    lax.fori_loop(0, n // 2 - 1, pair, 0)
    step(n - 2, 0)
    step(n - 1, 1, make_next=False)
    o = (acc_ref[0] / l_ref[0] - lam_ref[0] * (acc_ref[1] / l_ref[1])).T
    ms = jnp.mean(o * o, axis=-1, keepdims=True)
    o_ref[0] = (o * lax.rsqrt(ms + EPS) * sn_ref[...] * post_scale).astype(o_ref.dtype)


def diff_attention_core(qt, kp, vt, lam, sub_norm, post_scale, *, tq=512, tkc=1024):
    B, S, E = kp.shape
    W = 2 * HEAD_DIM
    H = E // W
    tq, tkc = min(tq, S), min(tkc, S // 2)
    assert S % (2 * tkc) == 0
    return pl.pallas_call(
        functools.partial(_attn_kernel, post_scale=post_scale, tkc=tkc),
        grid=(B, H, S // tq),
        in_specs=[
            pl.BlockSpec(memory_space=pltpu.SMEM),
            pl.BlockSpec((1, W, tq), lambda b, h, i: (b, h, i)),
            pl.BlockSpec((1, S, W), lambda b, h, i: (b, 0, h)),
            pl.BlockSpec((1, W, S), lambda b, h, i: (b, h, 0)),
            pl.BlockSpec((1, W), lambda b, h, i: (0, 0)),
        ],
        out_specs=pl.BlockSpec((1, tq, W), lambda b, h, i: (b, i, h)),
        out_shape=jax.ShapeDtypeStruct((B, S, E), BF16),
        scratch_shapes=[pltpu.VMEM((2, 1, tq), F32), pltpu.VMEM((2, 1, tq), F32), pltpu.VMEM((2, W, tq), F32)]
        + [pltpu.VMEM((tkc, tq), F32)] * 4 + [pltpu.VMEM((2, 2, 1, tq), F32)],
        compiler_params=_params(("parallel", "parallel", "arbitrary"), 56),
        name="diff_attention",
    )(lam.reshape(1), qt, kp, vt, sub_norm.reshape(1, W))


def _dft_tables(n):
    ang = 2.0 * np.pi * ((np.arange(n)[:, None] * np.arange(n)[None, :]) % n) / n
    return jnp.asarray(np.cos(ang), F32), jnp.asarray(np.sin(ang), F32)


def _fnet_a_kernel(x_ref, kc_ref, ks_ref, tc_ref, ts_ref, ar_ref, ai_ref):
    n_hi, digits, ec = x_ref.shape[1], x_ref.shape[2], x_ref.shape[3]
    x = x_ref[0].reshape(n_hi * digits, ec).astype(BF16)
    ar = jnp.dot(kc_ref[...], x, preferred_element_type=F32)
    ai = -jnp.dot(ks_ref[...], x, preferred_element_type=F32)
    tc, ts = tc_ref[0], ts_ref[0]
    ar_ref[0] = (ar * tc + ai * ts).reshape(n_hi, digits, ec)
    ai_ref[0] = (ai * tc - ar * ts).reshape(n_hi, digits, ec)


def _fnet_b_kernel(ar_ref, ai_ref, c_ref, s_ref, cc_ref, sc_ref, perm_ref, gw_ref, y_ref, *, norm):
    c, s = c_ref[...], s_ref[...]
    digits, n_lo = ar_ref.shape[1], ar_ref.shape[2]
    G = gw_ref.shape[2]
    br, bi = [], []
    for j in range(digits):
        ar, ai = ar_ref[0, j].astype(BF16), ai_ref[0, j].astype(BF16)
        br.append((jnp.dot(c, ar, preferred_element_type=F32)
                   + jnp.dot(s, ai, preferred_element_type=F32)).astype(BF16))
        bi.append((jnp.dot(c, ai, preferred_element_type=F32)
                   - jnp.dot(s, ar, preferred_element_type=F32)).astype(BF16))
    br, bi = jnp.concatenate(br, axis=0), jnp.concatenate(bi, axis=0)
    for g in range(gw_ref.shape[0]):
        sl = slice(g * G, (g + 1) * G)
        f = (jnp.dot(br[:, sl], cc_ref[...], preferred_element_type=F32)
             + jnp.dot(bi[:, sl], sc_ref[...], preferred_element_type=F32)) * norm
        f = jnp.dot(perm_ref[...], f.astype(BF16), preferred_element_type=F32).astype(BF16)
        y_ref[0, :, :, sl] = jnp.dot(f, gw_ref[g], preferred_element_type=F32).reshape(n_lo, digits, G)


def fourier_core(u, group_w, *, n_lo=128, e_chunk=512):
    B, S, N = u.shape
    E = N // 2
    G = E // FN_GROUPS
    n_hi = S // n_lo
    Ec = min(e_chunk, E)
    d = SUBLANES
    assert n_hi % d == 0 and n_lo % d == 0
    c_hi, s_hi = _dft_tables(n_hi)
    c_lo, s_lo = _dft_tables(n_lo)
    c_ch, s_ch = _dft_tables(G)
    eye = jnp.eye(d, dtype=F32)
    ang = 2.0 * np.pi * ((np.arange(n_lo)[:, None] * np.arange(n_hi)[None, :]) % S) / S
    tw = lambda t: jnp.asarray(t, F32).reshape(n_lo // d, d, n_hi).transpose(0, 2, 1).reshape(n_lo // d, n_hi * d, 1)
    perm = np.zeros((n_lo, d, d, n_lo), np.float32)
    perm[np.arange(n_lo)[:, None], np.arange(d)[None, :], np.arange(d)[None, :], np.arange(n_lo)[:, None]] = 1.0
    mat = lambda n: pl.BlockSpec((n, n), lambda b, j, e: (0, 0))
    tws = pl.BlockSpec((1, n_hi * d, 1), lambda b, j, e: (j, 0, 0))
    a_blk = pl.BlockSpec((1, n_hi, d, Ec), lambda b, j, e: (b, 0, j, e))
    ar, ai = pl.pallas_call(
        _fnet_a_kernel,
        grid=(B, n_lo // d, E // Ec),
        in_specs=[a_blk, mat(n_hi * d), mat(n_hi * d), tws, tws],
        out_specs=[a_blk, a_blk],
        out_shape=[jax.ShapeDtypeStruct((B, n_hi, n_lo, E), F32)] * 2,
        compiler_params=_params(("parallel", "parallel", "parallel"), 40),
        name="fnet_stage_a",
    )(u.reshape(B, n_hi, n_lo, N), jnp.kron(c_hi, eye).astype(BF16), jnp.kron(s_hi, eye).astype(BF16),
      tw(np.cos(ang)), tw(np.sin(ang)))

    a_in = pl.BlockSpec((1, d, n_lo, Ec), lambda b, j, e: (b, j, 0, e))
    y = pl.pallas_call(
        functools.partial(_fnet_b_kernel, norm=1.0 / math.sqrt(S * G)),
        grid=(B, n_hi // d, E // Ec),
        in_specs=[a_in, a_in, mat(n_lo), mat(n_lo), mat(G), mat(G), mat(n_lo * d),
                  pl.BlockSpec((Ec // G, G, G), lambda b, j, e: (e, 0, 0))],
        out_specs=pl.BlockSpec((1, n_lo, d, Ec), lambda b, j, e: (b, 0, j, e)),
        out_shape=jax.ShapeDtypeStruct((B, n_lo, n_hi, E), F32),
        compiler_params=_params(("parallel", "parallel", "parallel"), 40),
        name="fnet_stage_b",
    )(ar, ai, c_lo.astype(BF16), s_lo.astype(BF16), c_ch.astype(BF16), s_ch.astype(BF16),
      jnp.asarray(perm.reshape(n_lo * d, d * n_lo), BF16), group_w.astype(BF16))
    return y.reshape(B, S, E)


def conv_layer(x, norm, w_in, dw, dw_b, ln_g, ln_b, w_out):
    B, S, D = x.shape
    E = w_out.shape[0]
    u = norm_matmul(x.reshape(B * S, D), norm, w_in)
    o = conv_mixer(u.reshape(B, S, 3 * E), dw, dw_b, ln_g, ln_b)
    return gate_out(o.reshape(B * S, E), u, 2, x.reshape(B * S, D), w_out).reshape(B, S, D)


def hgrn_layer(x, norm, w_in, lb_fwd, lb_bwd, o_norm, w_out):
    B, S, D = x.shape
    E = w_out.shape[0]
    u = norm_matmul(x.reshape(B * S, D), norm, w_in)
    u3 = u.reshape(B, S, 5 * E)
    o_fwd = hgrn_direction(u3, lb_fwd, 1, reverse=False)
    o = hgrn_direction(u3, lb_bwd, 2, reverse=True, prev=o_fwd, o_norm=o_norm)
    return gate_out(o.reshape(B * S, E), u, 4, x.reshape(B * S, D), w_out).reshape(B, S, D)


def rope_tables(positions):
    half = ROPE_DIM // 2
    inv = 1.0 / (ROPE_THETA ** (jnp.arange(0, ROPE_DIM, 2, dtype=F32) / ROPE_DIM))
    ang = positions.astype(F32).reshape(-1, 1) * inv
    cos, sin = jnp.cos(ang), jnp.sin(ang)
    T = ang.shape[0]
    rest = LANES - ROPE_DIM
    keep = jnp.concatenate([cos, cos, jnp.ones((T, rest), F32)], axis=-1)
    partner = jnp.concatenate([-sin, sin, jnp.zeros((T, rest), F32)], axis=-1)
    return (cos.T, sin.T), (keep, partner)


def diff_layer(x, rope, norm, w_in, q_norm, k_norm, lam_q1, lam_k1, lam_q2, lam_k2, sub_norm, w_out, lam_init):
    B, S, D = x.shape
    E = w_out.shape[0]
    wq, wk, wv, wz = (w_in[:, i * E:(i + 1) * E] for i in range(4))
    qt = attn_projection(x, norm, wq, "q", rope[0], q_norm)
    kp = attn_projection(x, norm, wk, "k", rope[1], k_norm)
    vt = attn_projection(x, norm, wv, "v")
    z = norm_matmul(x.reshape(B * S, D), norm, wz)
    lam = jnp.exp(jnp.sum(lam_q1 * lam_k1)) - jnp.exp(jnp.sum(lam_q2 * lam_k2)) + lam_init
    o = diff_attention_core(qt, kp, vt, lam, sub_norm, 1.0 - lam_init)
    return gate_out(o.reshape(B * S, E), z, 0, x.reshape(B * S, D), w_out).reshape(B, S, D)


def fnet_layer(x, norm, w_in, group_w, w_out):
    B, S, D = x.shape
    E = w_out.shape[0]
    u = norm_matmul(x.reshape(B * S, D), norm, w_in)
    y = fourier_core(u.reshape(B, S, 2 * E), group_w)
    return gate_out(y.reshape(B * S, E), u, 1, x.reshape(B * S, D), w_out).reshape(B, S, D)


def hgrn_lower_bounds(table):
    lb = jnp.cumsum(jax.nn.softmax(table.astype(F32), axis=0), axis=0)
    return lb - lb[0:1]


def kernel(x, positions, conv_norm, conv_w_in, conv_dw, conv_dw_b, conv_ln_g, conv_ln_b, conv_w_out, hgrn_norm, hgrn_w_in, hgrn_lb_fwd, hgrn_lb_bwd, hgrn_o_norm, hgrn_w_out, diff_norm, diff_w_in, diff_q_norm, diff_k_norm, diff_lam_q1, diff_lam_k1, diff_lam_q2, diff_lam_k2, diff_sub_norm, diff_w_out, fnet_norm, fnet_w_in, fnet_group_w, fnet_w_out):
    depth = hgrn_lb_fwd.shape[0]
    n_mixers = 4
    rope = rope_tables(positions)
    lb_fwd = hgrn_lower_bounds(hgrn_lb_fwd)
    lb_bwd = hgrn_lower_bounds(hgrn_lb_bwd)
    for layer in range(depth):
        m, j = layer % n_mixers, layer // n_mixers
        if m == 0:
            x = conv_layer(x, conv_norm[j], conv_w_in[j], conv_dw[j], conv_dw_b[j], conv_ln_g[j], conv_ln_b[j],
                           conv_w_out[j])
        elif m == 1:
            x = hgrn_layer(x, hgrn_norm[j], hgrn_w_in[j], lb_fwd[layer], lb_bwd[layer], hgrn_o_norm[j],
                           hgrn_w_out[j])
        elif m == 2:
            lam_init = 0.8 - 0.6 * math.exp(-0.3 * layer)
            x = diff_layer(x, rope, diff_norm[j], diff_w_in[j], diff_q_norm[j], diff_k_norm[j], diff_lam_q1[j],
                           diff_lam_k1[j], diff_lam_q2[j], diff_lam_k2[j], diff_sub_norm[j], diff_w_out[j], lam_init)
        else:
            x = fnet_layer(x, fnet_norm[j], fnet_w_in[j], fnet_group_w[j], fnet_w_out[j])
    return x
```

```python
import functools
import math

import numpy as np
import jax
import jax.numpy as jnp
from jax import lax
from jax.experimental import pallas as pl
from jax.experimental.pallas import tpu as pltpu

EPS = 1e-6
LANES = 128
SUBLANES = 8
HEAD_DIM = 128
CONV_HALO = 16
ROPE_DIM = 32
ROPE_THETA = 500000.0
FN_GROUPS = 8
MIB = 1 << 20

F32 = jnp.float32
BF16 = jnp.bfloat16
NT_DIMS = (((1,), (1,)), ((), ()))


def _params(semantics, vmem_mib):
    return pltpu.CompilerParams(dimension_semantics=semantics, vmem_limit_bytes=vmem_mib * MIB)


def _silu(z):
    return z * jax.nn.sigmoid(z)


def _norm_matmul_kernel(x_ref, g_ref, w_ref, o_ref, h_ref):
    @pl.when(pl.program_id(1) == 0)
    def _():
        x = x_ref[...]
        ms = jnp.mean(x * x, axis=-1, keepdims=True)
        h_ref[...] = (x * lax.rsqrt(ms + EPS) * g_ref[...]).astype(h_ref.dtype)

    o_ref[...] = jnp.dot(h_ref[...], w_ref[...], preferred_element_type=F32).astype(o_ref.dtype)


def norm_matmul(x2d, gain, w, *, tm=1024, tn=1024):
    T, D = x2d.shape
    N = w.shape[1]
    tm, tn = min(tm, T), min(tn, N)
    return pl.pallas_call(
        _norm_matmul_kernel,
        grid=(T // tm, N // tn),
        in_specs=[
            pl.BlockSpec((tm, D), lambda i, j: (i, 0)),
            pl.BlockSpec((1, D), lambda i, j: (0, 0)),
            pl.BlockSpec((D, tn), lambda i, j: (0, j)),
        ],
        out_specs=pl.BlockSpec((tm, tn), lambda i, j: (i, j)),
        out_shape=jax.ShapeDtypeStruct((T, N), F32),
        scratch_shapes=[pltpu.VMEM((tm, D), BF16)],
        compiler_params=_params(("parallel", "arbitrary"), 48),
        name="norm_matmul",
    )(x2d, gain.reshape(1, D), w.astype(BF16))


def _gate_out_kernel(o_ref, z_ref, x_ref, w_ref, out_ref):
    g = (o_ref[...] * _silu(z_ref[...])).astype(BF16)
    out_ref[...] = x_ref[...] + jnp.dot(g, w_ref[...], preferred_element_type=F32)


def gate_out(o2d, u2d, z_block, x2d, w_out, *, tm=256):
    T, E = o2d.shape
    D = w_out.shape[1]
    tm = min(tm, T)
    return pl.pallas_call(
        _gate_out_kernel,
        grid=(T // tm,),
        in_specs=[
            pl.BlockSpec((tm, E), lambda i: (i, 0)),
            pl.BlockSpec((tm, E), lambda i: (i, z_block)),
            pl.BlockSpec((tm, D), lambda i: (i, 0)),
            pl.BlockSpec((E, D), lambda i: (0, 0)),
        ],
        out_specs=pl.BlockSpec((tm, D), lambda i: (i, 0)),
        out_shape=jax.ShapeDtypeStruct((T, D), F32),
        compiler_params=_params(("parallel",), 48),
        name="gate_out",
    )(o2d, u2d, x2d, w_out.astype(BF16))


def _conv_kernel(am_ref, ap_ref, an_ref, bm_ref, bp_ref, bn_ref, z_ref, x_ref, dw_ref, dwb_ref, g_ref, beta_ref,
                 w_ref, o_ref, vbuf, shifted, cbuf, *, taps, row_block, col_block):
    s = pl.program_id(1)
    ts, E = am_ref.shape[1], am_ref.shape[2]
    glu = lambda a, b: a * jax.nn.sigmoid(b)
    vbuf[0:CONV_HALO, :] = jnp.where(s > 0, glu(ap_ref[0], bp_ref[0]), 0.0)
    vbuf[CONV_HALO:CONV_HALO + ts, :] = glu(am_ref[0], bm_ref[0])
    vbuf[CONV_HALO + ts:, :] = jnp.where(s < pl.num_programs(1) - 1, glu(an_ref[0], bn_ref[0]), 0.0)
    span = shifted.shape[1]
    for r in range(1, SUBLANES):
        shifted[r - 1] = vbuf[r:r + span, :]
    first = CONV_HALO - taps // 2
    for r0 in range(0, ts, row_block):
        for c0 in range(0, E, col_block):
            cols = slice(c0, c0 + col_block)
            acc = jnp.zeros((row_block, col_block), F32)
            for k in range(taps):
                phase, base = (first + k) % SUBLANES, r0 + (first + k) // SUBLANES * SUBLANES
                win = (vbuf[base:base + row_block, cols] if phase == 0
                       else shifted[phase - 1, base:base + row_block, cols])
                acc = acc + dw_ref[k:k + 1, cols] * win
            cbuf[r0:r0 + row_block, cols] = acc + dwb_ref[:, cols]
    c = cbuf[...]
    mu = jnp.mean(c, axis=-1, keepdims=True)
    cc = c - mu
    var = jnp.mean(cc * cc, axis=-1, keepdims=True)
    y = cc * lax.rsqrt(var + EPS) * g_ref[...] + beta_ref[...]
    gated = (_silu(y) * _silu(z_ref[0])).astype(BF16)
    o_ref[...] = x_ref[...] + jnp.dot(gated, w_ref[...], preferred_element_type=F32)


def conv_mixer(u, x2d, dw, dw_b, ln_g, ln_b, w_out, *, ts=256):
    B, S, N = u.shape
    E = N // 3
    D = w_out.shape[1]
    taps = dw.shape[0]
    ts = min(ts, S)
    hb = ts // CONV_HALO
    st = S // ts
    last_halo = S // CONV_HALO - 1
    main = lambda col: pl.BlockSpec((1, ts, E), lambda b, s: (b, s, col))
    prev = lambda col: pl.BlockSpec((1, CONV_HALO, E), lambda b, s: (b, jnp.maximum(s * hb - 1, 0), col))
    nxt = lambda col: pl.BlockSpec((1, CONV_HALO, E), lambda b, s: (b, jnp.minimum((s + 1) * hb, last_halo), col))
    vec = lambda rows: pl.BlockSpec((rows, E), lambda b, s: (0, 0))
    rows = pl.BlockSpec((ts, D), lambda b, s: (b * st + s, 0))
    return pl.pallas_call(
        functools.partial(_conv_kernel, taps=taps, row_block=32, col_block=512),
        grid=(B, st),
        in_specs=[main(0), prev(0), nxt(0), main(1), prev(1), nxt(1), main(2), rows, vec(taps), vec(1), vec(1), vec(1),
                  pl.BlockSpec((E, D), lambda b, s: (0, 0), pipeline_mode=pl.Buffered(1))],
        out_specs=rows,
        out_shape=jax.ShapeDtypeStruct((B * S, D), F32),
        scratch_shapes=[pltpu.VMEM((ts + 2 * CONV_HALO, E), F32),
                        pltpu.VMEM((SUBLANES - 1, ts + 2 * CONV_HALO - SUBLANES, E), F32),
                        pltpu.VMEM((ts, E), F32)],
        compiler_params=_params(("parallel", "parallel"), 58),
        name="conv_mixer",
    )(u, u, u, u, u, u, u, x2d, dw, dw_b.reshape(1, E), ln_g.reshape(1, E), ln_b.reshape(1, E), w_out.astype(BF16))


def _hgrn_kernel(*refs, reverse, finalize, chunk):
    if finalize:
        q_ref, a_ref, v_ref, lb_ref, prev_ref, gn_ref, o_ref, st_ref = refs
    else:
        q_ref, a_ref, v_ref, lb_ref, o_ref, st_ref = refs
    C = chunk
    n = q_ref.shape[1] // C
    heads = q_ref.shape[2] // HEAD_DIM

    @pl.when(pl.program_id(2) == 0)
    def _():
        st_ref[...] = jnp.zeros_like(st_ref)

    row = lax.broadcasted_iota(jnp.int32, (C, HEAD_DIM), 0)
    pair_xor = lax.broadcasted_iota(jnp.int32, (C, C), 0) ^ lax.broadcasted_iota(jnp.int32, (C, C), 1)
    tt, ss = lax.broadcasted_iota(jnp.int32, (C, C), 0), lax.broadcasted_iota(jnp.int32, (C, C), 1)
    pair_level = jnp.where((tt < ss) if reverse else (tt > ss),
                           (pltpu.bitcast(pair_xor.astype(F32), jnp.int32) >> 23) - 127, -1)

    def one_chunk(j, carry):
        r0 = pl.multiple_of((n - 1 - j if reverse else j) * C, C)
        rows = pl.ds(r0, C)
        q, kk, v, near, far, tot, scores = [], [], [], [], [], [], []
        for h in range(heads):
            lanes = slice(h * HEAD_DIM, (h + 1) * HEAD_DIM)
            a, lb = a_ref[0, rows, lanes], lb_ref[:, lanes]
            e = jnp.exp(-jnp.abs(a))
            r = 1.0 / (1.0 + e)
            er = e * r
            pos = a >= 0
            f = lb + (1.0 - lb) * jnp.where(pos, r, er)
            q.append(q_ref[0, rows, lanes])
            v.append(v_ref[0, rows, lanes])
            kk.append((1.0 - lb) * jnp.where(pos, er, r))
            near.append(f)
            far.append(jnp.ones_like(f))
            tot.append(f)
            scores.append(jnp.zeros((C, C), F32))

        m, level = 1, 0
        while m < C:
            bit = (row & m) != 0
            for h in range(heads):
                qm = (q[h] * near[h]).astype(BF16)
                km = (kk[h] * far[h]).astype(BF16)
                sc = lax.dot_general(qm, km, NT_DIMS, preferred_element_type=F32)
                scores[h] = jnp.where(pair_level == level, sc, scores[h])
                tot_before = pltpu.roll(tot[h], m, 0)
                tot_after = pltpu.roll(tot[h], C - m, 0)
                if reverse:
                    near[h] = jnp.where(bit, near[h], near[h] * tot_after)
                    far[h] = jnp.where(bit, far[h] * tot_before, far[h])
                else:
                    near[h] = jnp.where(bit, near[h] * tot_before, near[h])
                    far[h] = jnp.where(bit, far[h], far[h] * tot_after)
                tot[h] = tot[h] * jnp.where(bit, tot_before, tot_after)
            m *= 2
            level += 1

        for h in range(heads):
            lanes = slice(h * HEAD_DIM, (h + 1) * HEAD_DIM)
            vb = v[h].astype(BF16)
            st = st_ref[h]
            o = jnp.dot(scores[h].astype(BF16), vb, preferred_element_type=F32)
            o = o + jnp.sum(q[h] * kk[h], axis=1, keepdims=True) * v[h]
            o = o + lax.dot_general((q[h] * near[h]).astype(BF16), st.astype(BF16), NT_DIMS,
                                    preferred_element_type=F32)
            st_ref[h] = st * tot[h][0:1] + jnp.dot(v[h].T.astype(BF16), (kk[h] * far[h]).astype(BF16),
                                                   preferred_element_type=F32)
            if finalize:
                o = o + prev_ref[0, rows, lanes]
                ms = jnp.mean(o * o, axis=-1, keepdims=True)
                o = o * lax.rsqrt(ms + EPS) * gn_ref[...]
            o_ref[0, rows, lanes] = o.astype(o_ref.dtype)
        return carry

    lax.fori_loop(0, n, one_chunk, 0)


def hgrn_direction(u, lb, gate_block, *, reverse, prev=None, o_norm=None, rows=1024, chunk=128, heads=4):
    B, S, N = u.shape
    E = N // 5
    W = heads * HEAD_DIM
    HB = E // W
    R = min(rows, S)
    nb = S // R
    finalize = prev is not None
    ridx = (lambda c: nb - 1 - c) if reverse else (lambda c: c)
    col = lambda blk: pl.BlockSpec((1, R, W), lambda b, h, c: (b, ridx(c), blk * HB + h))
    in_specs = [col(0), col(gate_block), col(3), pl.BlockSpec((1, W), lambda b, h, c: (0, h))]
    args = [u, u, u, lb.reshape(1, E)]
    if finalize:
        in_specs += [pl.BlockSpec((1, R, W), lambda b, h, c: (b, ridx(c), h)),
                     pl.BlockSpec((1, HEAD_DIM), lambda b, h, c: (0, 0))]
        args += [prev, o_norm.reshape(1, HEAD_DIM)]
    return pl.pallas_call(
        functools.partial(_hgrn_kernel, reverse=reverse, finalize=finalize, chunk=min(chunk, R)),
        grid=(B, HB, nb),
        in_specs=in_specs,
        out_specs=pl.BlockSpec((1, R, W), lambda b, h, c: (b, ridx(c), h)),
        out_shape=jax.ShapeDtypeStruct((B, S, E), BF16 if finalize else F32),
        scratch_shapes=[pltpu.VMEM((heads, HEAD_DIM, HEAD_DIM), F32)],
        compiler_params=_params(("parallel", "parallel", "arbitrary"), 40),
        name="hgrn_bwd" if reverse else "hgrn_fwd",
    )(*args)


def _attn_proj_kernel(x_ref, g_ref, w_ref, *rest, mode, scale, slab):
    if mode == "v":
        o_ref, h_ref = rest
    else:
        t1_ref, t2_ref, gn_ref, *rest = rest
        if mode == "k":
            swap_ref, o_ref, h_ref = rest
        else:
            o_ref, h_ref = rest

    @pl.when(pl.program_id(1) == 0)
    def _():
        x = x_ref[...]
        ms = jnp.mean(x * x, axis=-1, keepdims=True)
        h_ref[...] = (x * lax.rsqrt(ms + EPS) * g_ref[...]).astype(h_ref.dtype)

    half = ROPE_DIM // 2
    project = lambda c0: jnp.dot(h_ref[...], w_ref[:, c0:c0 + slab], preferred_element_type=F32)
    tn = w_ref.shape[1]
    nxt = project(0)
    for c0 in range(0, tn, slab):
        acc, nxt = nxt, (project(c0 + slab) if c0 + slab < tn else None)
        for g0 in range(0, slab, HEAD_DIM):
            y = acc[:, g0:g0 + HEAD_DIM]
            cols = slice(c0 + g0, c0 + g0 + HEAD_DIM)
            if mode == "q":
                yt = y.T
                yt = yt * lax.rsqrt(jnp.mean(yt * yt, axis=0, keepdims=True) + EPS) * (gn_ref[...] * scale)
                lo, hi, cos, sin = yt[:half], yt[half:ROPE_DIM], t1_ref[...], t2_ref[...]
                yt = jnp.concatenate([lo * cos - hi * sin, hi * cos + lo * sin, yt[ROPE_DIM:]], axis=0)
                o_ref[0, cols, :] = yt.astype(o_ref.dtype)
            elif mode == "k":
                yn = y * lax.rsqrt(jnp.mean(y * y, axis=-1, keepdims=True) + EPS) * gn_ref[...]
                partner = jnp.dot(yn.astype(BF16), swap_ref[...], preferred_element_type=F32)
                o_ref[:, cols] = (yn * t1_ref[...] + partner * t2_ref[...]).astype(o_ref.dtype)
            else:
                o_ref[0, cols, :] = y.T.astype(o_ref.dtype)


def attn_projection(x, gain, w, mode, tables=None, head_gain=None, *, tm=1024, tn=1024):
    B, S, D = x.shape
    E = w.shape[1]
    tm, tn = min(tm, S), min(tn, E)
    sb = S // tm
    if mode == "k":
        out_spec = pl.BlockSpec((tm, tn), lambda i, j: (i, j))
        out_shape = jax.ShapeDtypeStruct((B * S, E), BF16)
    else:
        out_spec = pl.BlockSpec((1, tn, tm), lambda i, j: (i // sb, j, i % sb))
        out_shape = jax.ShapeDtypeStruct((B, E, S), BF16)
    in_specs = [
        pl.BlockSpec((tm, D), lambda i, j: (i, 0)),
        pl.BlockSpec((1, D), lambda i, j: (0, 0)),
        pl.BlockSpec((D, tn), lambda i, j: (0, j)),
    ]
    args = [x.reshape(B * S, D), gain.reshape(1, D), w.astype(BF16)]
    half = ROPE_DIM // 2
    if mode == "q":
        in_specs += [pl.BlockSpec((half, tm), lambda i, j: (0, i))] * 2 + [pl.BlockSpec((HEAD_DIM, 1), lambda i, j: (0, 0))]
        args += [*tables, head_gain.reshape(HEAD_DIM, 1)]
    elif mode == "k":
        swap = np.zeros((HEAD_DIM, HEAD_DIM), np.float32)
        swap[np.arange(half) + half, np.arange(half)] = 1.0
        swap[np.arange(half), np.arange(half) + half] = 1.0
        in_specs += [pl.BlockSpec((tm, LANES), lambda i, j: (i, 0))] * 2 + [
            pl.BlockSpec((1, HEAD_DIM), lambda i, j: (0, 0)), pl.BlockSpec((HEAD_DIM, HEAD_DIM), lambda i, j: (0, 0))]
        args += [*tables, head_gain.reshape(1, HEAD_DIM), jnp.asarray(swap, BF16)]
    out = pl.pallas_call(
        functools.partial(_attn_proj_kernel, mode=mode, scale=math.log2(math.e) / math.sqrt(HEAD_DIM),
                          slab=min(2 * HEAD_DIM, tn)),
        grid=(B * S // tm, E // tn),
        in_specs=in_specs,
        out_specs=out_spec,
        out_shape=out_shape,
        scratch_shapes=[pltpu.VMEM((tm, D), BF16)],
        compiler_params=_params(("parallel", "arbitrary"), 48),
        name="attn_proj_" + mode,
    )(*args)
    return out.reshape(B, S, E) if mode == "k" else out


def _attn_kernel(lam_ref, qt_ref, k_ref, vt_ref, sn_ref, o_ref, m_ref, l_ref, acc_ref, s00, s01, s10, s11, mc_ref,
                 *, post_scale, tq, tkc):
    tile = pl.program_id(2)
    n = k_ref.shape[1] // tkc
    s_refs = ((s00, s01), (s10, s11))
    m_ref[...] = jnp.full_like(m_ref, -jnp.inf)
    l_ref[...] = jnp.zeros_like(l_ref)
    acc_ref[...] = jnp.zeros_like(acc_ref)

    def produce(q_tile, c, slot, i):
        sl = slice(i * HEAD_DIM, (i + 1) * HEAD_DIM)
        off = pl.multiple_of(c * tkc, tkc)
        q = qt_ref[0, sl, pl.ds(pl.multiple_of(q_tile * tq, tq), tq)]
        s_new = jnp.dot(k_ref[0, pl.ds(off, tkc), sl], q, preferred_element_type=F32)
        s_refs[slot][i][...] = s_new
        mc_ref[slot, i] = jnp.max(s_new, axis=0, keepdims=True)

    def consume(vt_c, slot, i):
        m_prev = m_ref[i]
        m_new = jnp.maximum(m_prev, mc_ref[slot, i])
        alpha = jnp.exp2(m_prev - m_new)
        p = jnp.exp2(s_refs[slot][i][...] - m_new)
        l_ref[i] = alpha * l_ref[i] + jnp.sum(p, axis=0, keepdims=True)
        acc_ref[i] = alpha * acc_ref[i] + jnp.dot(vt_c, p.astype(BF16), preferred_element_type=F32)
        m_ref[i] = m_new

    def step(c, slot, next_tile, next_chunk):
        vt_c = vt_ref[0, :, pl.ds(pl.multiple_of(c * tkc, tkc), tkc)]
        for i in range(2):
            produce(next_tile, next_chunk, 1 - slot, i)
            consume(vt_c, slot, i)

    @pl.when(tile == 0)
    def _():
        produce(0, 0, 0, 0)
        produce(0, 0, 0, 1)

    def pair(j, carry):
        step(2 * j, 0, tile, 2 * j + 1)
        step(2 * j + 1, 1, tile, 2 * j + 2)
        return carry

    lax.fori_loop(0, n // 2 - 1, pair, 0)
    step(n - 2, 0, tile, n - 1)
    step(n - 1, 1, jnp.minimum(tile + 1, pl.num_programs(2) - 1), 0)
    o = (acc_ref[0] / l_ref[0] - lam_ref[0] * (acc_ref[1] / l_ref[1])).T
    ms = jnp.mean(o * o, axis=-1, keepdims=True)
    o_ref[0] = (o * lax.rsqrt(ms + EPS) * sn_ref[...] * post_scale).astype(o_ref.dtype)


def diff_attention_core(qt, kp, vt, lam, sub_norm, post_scale, *, tq=512, tkc=1024):
    B, S, E = kp.shape
    W = 2 * HEAD_DIM
    H = E // W
    tq, tkc = min(tq, S), min(tkc, S // 2)
    assert S % (2 * tkc) == 0
    return pl.pallas_call(
        functools.partial(_attn_kernel, post_scale=post_scale, tq=tq, tkc=tkc),
        grid=(B, H, S // tq),
        in_specs=[
            pl.BlockSpec(memory_space=pltpu.SMEM),
            pl.BlockSpec((1, W, S), lambda b, h, i: (b, h, 0), pipeline_mode=pl.Buffered(1)),
            pl.BlockSpec((1, S, W), lambda b, h, i: (b, 0, h)),
            pl.BlockSpec((1, W, S), lambda b, h, i: (b, h, 0)),
            pl.BlockSpec((1, W), lambda b, h, i: (0, 0)),
        ],
        out_specs=pl.BlockSpec((1, tq, W), lambda b, h, i: (b, i, h)),
        out_shape=jax.ShapeDtypeStruct((B, S, E), BF16),
        scratch_shapes=[pltpu.VMEM((2, 1, tq), F32), pltpu.VMEM((2, 1, tq), F32), pltpu.VMEM((2, W, tq), F32)]
        + [pltpu.VMEM((tkc, tq), F32)] * 4 + [pltpu.VMEM((2, 2, 1, tq), F32)],
        compiler_params=_params(("parallel", "parallel", "arbitrary"), 56),
        name="diff_attention",
    )(lam.reshape(1), qt, kp, vt, sub_norm.reshape(1, W))


def _dft_tables(n):
    ang = 2.0 * np.pi * ((np.arange(n)[:, None] * np.arange(n)[None, :]) % n) / n
    return jnp.asarray(np.cos(ang), F32), jnp.asarray(np.sin(ang), F32)


def _fnet_a_kernel(x_ref, kc_ref, ks_ref, tc_ref, ts_ref, ar_ref, ai_ref):
    n_hi, digits, ec = x_ref.shape[1], x_ref.shape[2], x_ref.shape[3]
    x = x_ref[0].reshape(n_hi * digits, ec).astype(BF16)
    ar = jnp.dot(kc_ref[...], x, preferred_element_type=F32)
    ai = -jnp.dot(ks_ref[...], x, preferred_element_type=F32)
    tc, ts = tc_ref[0], ts_ref[0]
    ar_ref[0] = (ar * tc + ai * ts).reshape(n_hi, digits, ec)
    ai_ref[0] = (ai * tc - ar * ts).reshape(n_hi, digits, ec)


def _fnet_b_kernel(ar_ref, ai_ref, c_ref, s_ref, cc_ref, sc_ref, perm_ref, gw_ref, y_ref, *, norm):
    c, s = c_ref[...], s_ref[...]
    digits, n_lo = ar_ref.shape[1], ar_ref.shape[2]
    G = gw_ref.shape[2]
    br, bi = [], []
    for j in range(digits):
        ar, ai = ar_ref[0, j].astype(BF16), ai_ref[0, j].astype(BF16)
        br.append((jnp.dot(c, ar, preferred_element_type=F32)
                   + jnp.dot(s, ai, preferred_element_type=F32)).astype(BF16))
        bi.append((jnp.dot(c, ai, preferred_element_type=F32)
                   - jnp.dot(s, ar, preferred_element_type=F32)).astype(BF16))
    br, bi = jnp.concatenate(br, axis=0), jnp.concatenate(bi, axis=0)
    for g in range(gw_ref.shape[0]):
        sl = slice(g * G, (g + 1) * G)
        f = (jnp.dot(br[:, sl], cc_ref[...], preferred_element_type=F32)
             + jnp.dot(bi[:, sl], sc_ref[...], preferred_element_type=F32)) * norm
        f = jnp.dot(perm_ref[...], f.astype(BF16), preferred_element_type=F32).astype(BF16)
        y_ref[0, :, :, sl] = jnp.dot(f, gw_ref[g], preferred_element_type=F32).reshape(n_lo, digits, G)


def fourier_core(u, group_w, *, n_lo=128, e_chunk=512):
    B, S, N = u.shape
    E = N // 2
    G = E // FN_GROUPS
    n_hi = S // n_lo
    Ec = min(e_chunk, E)
    d = SUBLANES
    assert n_hi % d == 0 and n_lo % d == 0
    c_hi, s_hi = _dft_tables(n_hi)
    c_lo, s_lo = _dft_tables(n_lo)
    c_ch, s_ch = _dft_tables(G)
    eye = jnp.eye(d, dtype=F32)
    ang = 2.0 * np.pi * ((np.arange(n_lo)[:, None] * np.arange(n_hi)[None, :]) % S) / S
    tw = lambda t: jnp.asarray(t, F32).reshape(n_lo // d, d, n_hi).transpose(0, 2, 1).reshape(n_lo // d, n_hi * d, 1)
    perm = np.zeros((n_lo, d, d, n_lo), np.float32)
    perm[np.arange(n_lo)[:, None], np.arange(d)[None, :], np.arange(d)[None, :], np.arange(n_lo)[:, None]] = 1.0
    mat = lambda n: pl.BlockSpec((n, n), lambda b, j, e: (0, 0))
    tws = pl.BlockSpec((1, n_hi * d, 1), lambda b, j, e: (j, 0, 0))
    a_blk = pl.BlockSpec((1, n_hi, d, Ec), lambda b, j, e: (b, 0, j, e))
    ar, ai = pl.pallas_call(
        _fnet_a_kernel,
        grid=(B, n_lo // d, E // Ec),
        in_specs=[a_blk, mat(n_hi * d), mat(n_hi * d), tws, tws],
        out_specs=[a_blk, a_blk],
        out_shape=[jax.ShapeDtypeStruct((B, n_hi, n_lo, E), F32)] * 2,
        compiler_params=_params(("parallel", "parallel", "parallel"), 40),
        name="fnet_stage_a",
    )(u.reshape(B, n_hi, n_lo, N), jnp.kron(c_hi, eye).astype(BF16), jnp.kron(s_hi, eye).astype(BF16),
      tw(np.cos(ang)), tw(np.sin(ang)))

    a_in = pl.BlockSpec((1, d, n_lo, Ec), lambda b, j, e: (b, j, 0, e))
    y = pl.pallas_call(
        functools.partial(_fnet_b_kernel, norm=1.0 / math.sqrt(S * G)),
        grid=(B, n_hi // d, E // Ec),
        in_specs=[a_in, a_in, mat(n_lo), mat(n_lo), mat(G), mat(G), mat(n_lo * d),
                  pl.BlockSpec((Ec // G, G, G), lambda b, j, e: (e, 0, 0))],
        out_specs=pl.BlockSpec((1, n_lo, d, Ec), lambda b, j, e: (b, 0, j, e)),
        out_shape=jax.ShapeDtypeStruct((B, n_lo, n_hi, E), F32),
        compiler_params=_params(("parallel", "parallel", "parallel"), 40),
        name="fnet_stage_b",
    )(ar, ai, c_lo.astype(BF16), s_lo.astype(BF16), c_ch.astype(BF16), s_ch.astype(BF16),
      jnp.asarray(perm.reshape(n_lo * d, d * n_lo), BF16), group_w.astype(BF16))
    return y.reshape(B, S, E)


def conv_layer(x, norm, w_in, dw, dw_b, ln_g, ln_b, w_out):
    B, S, D = x.shape
    E = w_out.shape[0]
    u = norm_matmul(x.reshape(B * S, D), norm, w_in)
    return conv_mixer(u.reshape(B, S, 3 * E), x.reshape(B * S, D), dw, dw_b, ln_g, ln_b, w_out).reshape(B, S, D)


def hgrn_layer(x, norm, w_in, lb_fwd, lb_bwd, o_norm, w_out):
    B, S, D = x.shape
    E = w_out.shape[0]
    u = norm_matmul(x.reshape(B * S, D), norm, w_in)
    u3 = u.reshape(B, S, 5 * E)
    o_fwd = hgrn_direction(u3, lb_fwd, 1, reverse=False)
    o = hgrn_direction(u3, lb_bwd, 2, reverse=True, prev=o_fwd, o_norm=o_norm)
    return gate_out(o.reshape(B * S, E), u, 4, x.reshape(B * S, D), w_out).reshape(B, S, D)


def rope_tables(positions):
    half = ROPE_DIM // 2
    inv = 1.0 / (ROPE_THETA ** (jnp.arange(0, ROPE_DIM, 2, dtype=F32) / ROPE_DIM))
    ang = positions.astype(F32).reshape(-1, 1) * inv
    cos, sin = jnp.cos(ang), jnp.sin(ang)
    T = ang.shape[0]
    rest = LANES - ROPE_DIM
    keep = jnp.concatenate([cos, cos, jnp.ones((T, rest), F32)], axis=-1)
    partner = jnp.concatenate([-sin, sin, jnp.zeros((T, rest), F32)], axis=-1)
    return (cos.T, sin.T), (keep, partner)


def diff_layer(x, rope, norm, w_in, q_norm, k_norm, lam_q1, lam_k1, lam_q2, lam_k2, sub_norm, w_out, lam_init):
    B, S, D = x.shape
    E = w_out.shape[0]
    wq, wk, wv, wz = (w_in[:, i * E:(i + 1) * E] for i in range(4))
    qt = attn_projection(x, norm, wq, "q", rope[0], q_norm)
    kp = attn_projection(x, norm, wk, "k", rope[1], k_norm)
    vt = attn_projection(x, norm, wv, "v")
    z = norm_matmul(x.reshape(B * S, D), norm, wz)
    lam = jnp.exp(jnp.sum(lam_q1 * lam_k1)) - jnp.exp(jnp.sum(lam_q2 * lam_k2)) + lam_init
    o = diff_attention_core(qt, kp, vt, lam, sub_norm, 1.0 - lam_init)
    return gate_out(o.reshape(B * S, E), z, 0, x.reshape(B * S, D), w_out).reshape(B, S, D)


def fnet_layer(x, norm, w_in, group_w, w_out):
    B, S, D = x.shape
    E = w_out.shape[0]
    u = norm_matmul(x.reshape(B * S, D), norm, w_in)
    y = fourier_core(u.reshape(B, S, 2 * E), group_w)
    return gate_out(y.reshape(B * S, E), u, 1, x.reshape(B * S, D), w_out).reshape(B, S, D)


def hgrn_lower_bounds(table):
    lb = jnp.cumsum(jax.nn.softmax(table.astype(F32), axis=0), axis=0)
    return lb - lb[0:1]


def kernel(x, positions, conv_norm, conv_w_in, conv_dw, conv_dw_b, conv_ln_g, conv_ln_b, conv_w_out, hgrn_norm, hgrn_w_in, hgrn_lb_fwd, hgrn_lb_bwd, hgrn_o_norm, hgrn_w_out, diff_norm, diff_w_in, diff_q_norm, diff_k_norm, diff_lam_q1, diff_lam_k1, diff_lam_q2, diff_lam_k2, diff_sub_norm, diff_w_out, fnet_norm, fnet_w_in, fnet_group_w, fnet_w_out):
    depth = hgrn_lb_fwd.shape[0]
    n_mixers = 4
    rope = rope_tables(positions)
    lb_fwd = hgrn_lower_bounds(hgrn_lb_fwd)
    lb_bwd = hgrn_lower_bounds(hgrn_lb_bwd)
    for layer in range(depth):
        m, j = layer % n_mixers, layer // n_mixers
        if m == 0:
            x = conv_layer(x, conv_norm[j], conv_w_in[j], conv_dw[j], conv_dw_b[j], conv_ln_g[j], conv_ln_b[j],
                           conv_w_out[j])
        elif m == 1:
            x = hgrn_layer(x, hgrn_norm[j], hgrn_w_in[j], lb_fwd[layer], lb_bwd[layer], hgrn_o_norm[j],
                           hgrn_w_out[j])
        elif m == 2:
            lam_init = 0.8 - 0.6 * math.exp(-0.3 * layer)
            x = diff_layer(x, rope, diff_norm[j], diff_w_in[j], diff_q_norm[j], diff_k_norm[j], diff_lam_q1[j],
                           diff_lam_k1[j], diff_lam_q2[j], diff_lam_k2[j], diff_sub_norm[j], diff_w_out[j], lam_init)
        else:
            x = fnet_layer(x, fnet_norm[j], fnet_w_in[j], fnet_group_w[j], fnet_w_out[j])
    return x
```

```python
import functools
import math

import numpy as np
import jax
import jax.numpy as jnp
from jax import lax
from jax.experimental import pallas as pl
from jax.experimental.pallas import tpu as pltpu

EPS = 1e-6
LANES = 128
SUBLANES = 8
HEAD_DIM = 128
CONV_HALO = 16
ROPE_DIM = 32
ROPE_THETA = 500000.0
FN_GROUPS = 8
MIB = 1 << 20

F32 = jnp.float32
BF16 = jnp.bfloat16
NT_DIMS = (((1,), (1,)), ((), ()))


def _params(semantics, vmem_mib):
    return pltpu.CompilerParams(dimension_semantics=semantics, vmem_limit_bytes=vmem_mib * MIB)


def _silu(z):
    return z * jax.nn.sigmoid(z)


def _norm_matmul_kernel(x_ref, g_ref, w_ref, o_ref, h_ref):
    @pl.when(pl.program_id(1) == 0)
    def _():
        x = x_ref[...]
        ms = jnp.mean(x * x, axis=-1, keepdims=True)
        h_ref[...] = (x * lax.rsqrt(ms + EPS) * g_ref[...]).astype(h_ref.dtype)

    o_ref[...] = jnp.dot(h_ref[...], w_ref[...], preferred_element_type=F32).astype(o_ref.dtype)


def norm_matmul(x2d, gain, w, *, tm=1024, tn=2048):
    T, D = x2d.shape
    N = w.shape[1]
    tm, tn = min(tm, T), min(tn, N)
    return pl.pallas_call(
        _norm_matmul_kernel,
        grid=(T // tm, N // tn),
        in_specs=[
            pl.BlockSpec((tm, D), lambda i, j: (i, 0)),
            pl.BlockSpec((1, D), lambda i, j: (0, 0)),
            pl.BlockSpec((D, tn), lambda i, j: (0, j)),
        ],
        out_specs=pl.BlockSpec((tm, tn), lambda i, j: (i, j)),
        out_shape=jax.ShapeDtypeStruct((T, N), F32),
        scratch_shapes=[pltpu.VMEM((tm, D), BF16)],
        compiler_params=_params(("parallel", "arbitrary"), 58),
        name="norm_matmul",
    )(x2d, gain.reshape(1, D), w.astype(BF16))


def _gate_out_kernel(o_ref, z_ref, x_ref, w_ref, out_ref):
    g = (o_ref[...] * _silu(z_ref[...])).astype(BF16)
    out_ref[...] = x_ref[...] + jnp.dot(g, w_ref[...], preferred_element_type=F32)


def gate_out(o2d, u2d, z_block, x2d, w_out, *, tm=512):
    T, E = o2d.shape
    D = w_out.shape[1]
    tm = min(tm, T)
    return pl.pallas_call(
        _gate_out_kernel,
        grid=(T // tm,),
        in_specs=[
            pl.BlockSpec((tm, E), lambda i: (i, 0)),
            pl.BlockSpec((tm, E), lambda i: (i, z_block)),
            pl.BlockSpec((tm, D), lambda i: (i, 0)),
            pl.BlockSpec((E, D), lambda i: (0, 0), pipeline_mode=pl.Buffered(1)),
        ],
        out_specs=pl.BlockSpec((tm, D), lambda i: (i, 0)),
        out_shape=jax.ShapeDtypeStruct((T, D), F32),
        compiler_params=_params(("parallel",), 48),
        name="gate_out",
    )(o2d, u2d, x2d, w_out.astype(BF16))


def _conv_kernel(am_ref, ap_ref, an_ref, bm_ref, bp_ref, bn_ref, z_ref, x_ref, dw_ref, dwb_ref, g_ref, beta_ref,
                 w_ref, o_ref, vbuf, shifted, cbuf, *, taps, row_block, col_block):
    s = pl.program_id(1)
    ts, E = am_ref.shape[1], am_ref.shape[2]
    glu = lambda a, b: a * jax.nn.sigmoid(b)
    vbuf[0:CONV_HALO, :] = jnp.where(s > 0, glu(ap_ref[0], bp_ref[0]), 0.0)
    vbuf[CONV_HALO:CONV_HALO + ts, :] = glu(am_ref[0], bm_ref[0])
    vbuf[CONV_HALO + ts:, :] = jnp.where(s < pl.num_programs(1) - 1, glu(an_ref[0], bn_ref[0]), 0.0)
    span = shifted.shape[1]
    for r in range(1, SUBLANES):
        shifted[r - 1] = vbuf[r:r + span, :]
    first = CONV_HALO - taps // 2
    for r0 in range(0, ts, row_block):
        for c0 in range(0, E, col_block):
            cols = slice(c0, c0 + col_block)
            acc = jnp.zeros((row_block, col_block), F32)
            for k in range(taps):
                phase, base = (first + k) % SUBLANES, r0 + (first + k) // SUBLANES * SUBLANES
                win = (vbuf[base:base + row_block, cols] if phase == 0
                       else shifted[phase - 1, base:base + row_block, cols])
                acc = acc + dw_ref[k:k + 1, cols] * win
            cbuf[r0:r0 + row_block, cols] = acc + dwb_ref[:, cols]
    c = cbuf[...]
    mu = jnp.mean(c, axis=-1, keepdims=True)
    cc = c - mu
    var = jnp.mean(cc * cc, axis=-1, keepdims=True)
    y = cc * lax.rsqrt(var + EPS) * g_ref[...] + beta_ref[...]
    gated = (_silu(y) * _silu(z_ref[0])).astype(BF16)
    o_ref[...] = x_ref[...] + jnp.dot(gated, w_ref[...], preferred_element_type=F32)


def conv_mixer(u, x2d, dw, dw_b, ln_g, ln_b, w_out, *, ts=256):
    B, S, N = u.shape
    E = N // 3
    D = w_out.shape[1]
    taps = dw.shape[0]
    ts = min(ts, S)
    hb = ts // CONV_HALO
    st = S // ts
    last_halo = S // CONV_HALO - 1
    main = lambda col: pl.BlockSpec((1, ts, E), lambda b, s: (b, s, col))
    prev = lambda col: pl.BlockSpec((1, CONV_HALO, E), lambda b, s: (b, jnp.maximum(s * hb - 1, 0), col))
    nxt = lambda col: pl.BlockSpec((1, CONV_HALO, E), lambda b, s: (b, jnp.minimum((s + 1) * hb, last_halo), col))
    vec = lambda rows: pl.BlockSpec((rows, E), lambda b, s: (0, 0))
    rows = pl.BlockSpec((ts, D), lambda b, s: (b * st + s, 0))
    return pl.pallas_call(
        functools.partial(_conv_kernel, taps=taps, row_block=32, col_block=512),
        grid=(B, st),
        in_specs=[main(0), prev(0), nxt(0), main(1), prev(1), nxt(1), main(2), rows, vec(taps), vec(1), vec(1), vec(1),
                  pl.BlockSpec((E, D), lambda b, s: (0, 0), pipeline_mode=pl.Buffered(1))],
        out_specs=rows,
        out_shape=jax.ShapeDtypeStruct((B * S, D), F32),
        scratch_shapes=[pltpu.VMEM((ts + 2 * CONV_HALO, E), F32),
                        pltpu.VMEM((SUBLANES - 1, ts + 2 * CONV_HALO - SUBLANES, E), F32),
                        pltpu.VMEM((ts, E), F32)],
        compiler_params=_params(("parallel", "parallel"), 58),
        name="conv_mixer",
    )(u, u, u, u, u, u, u, x2d, dw, dw_b.reshape(1, E), ln_g.reshape(1, E), ln_b.reshape(1, E), w_out.astype(BF16))


def _hgrn_kernel(*refs, reverse, finalize, chunk):
    if finalize:
        q_ref, a_ref, v_ref, lb_ref, prev_ref, gn_ref, o_ref, st_ref = refs
    else:
        q_ref, a_ref, v_ref, lb_ref, o_ref, st_ref = refs
    C = chunk
    n = q_ref.shape[1] // C
    heads = q_ref.shape[2] // HEAD_DIM

    @pl.when(pl.program_id(2) == 0)
    def _():
        st_ref[...] = jnp.zeros_like(st_ref)

    row = lax.broadcasted_iota(jnp.int32, (C, HEAD_DIM), 0)
    pair_xor = lax.broadcasted_iota(jnp.int32, (C, C), 0) ^ lax.broadcasted_iota(jnp.int32, (C, C), 1)
    tt, ss = lax.broadcasted_iota(jnp.int32, (C, C), 0), lax.broadcasted_iota(jnp.int32, (C, C), 1)
    pair_level = jnp.where((tt < ss) if reverse else (tt > ss),
                           (pltpu.bitcast(pair_xor.astype(F32), jnp.int32) >> 23) - 127, -1)

    def one_chunk(j, carry):
        r0 = pl.multiple_of((n - 1 - j if reverse else j) * C, C)
        rows = pl.ds(r0, C)
        q, kk, v, near, far, tot, scores = [], [], [], [], [], [], []
        for h in range(heads):
            lanes = slice(h * HEAD_DIM, (h + 1) * HEAD_DIM)
            a, lb = a_ref[0, rows, lanes], lb_ref[:, lanes]
            e = jnp.exp(-jnp.abs(a))
            r = 1.0 / (1.0 + e)
            er = e * r
            pos = a >= 0
            f = lb + (1.0 - lb) * jnp.where(pos, r, er)
            q.append(q_ref[0, rows, lanes])
            v.append(v_ref[0, rows, lanes])
            kk.append((1.0 - lb) * jnp.where(pos, er, r))
            near.append(f)
            far.append(jnp.ones_like(f))
            tot.append(f)
            scores.append(jnp.zeros((C, C), F32))

        m, level = 1, 0
        while m < C:
            bit = (row & m) != 0
            for h in range(heads):
                qm = (q[h] * near[h]).astype(BF16)
                km = (kk[h] * far[h]).astype(BF16)
                sc = lax.dot_general(qm, km, NT_DIMS, preferred_element_type=F32)
                scores[h] = jnp.where(pair_level == level, sc, scores[h])
                tot_before = pltpu.roll(tot[h], m, 0)
                tot_after = pltpu.roll(tot[h], C - m, 0)
                if reverse:
                    near[h] = jnp.where(bit, near[h], near[h] * tot_after)
                    far[h] = jnp.where(bit, far[h] * tot_before, far[h])
                else:
                    near[h] = jnp.where(bit, near[h] * tot_before, near[h])
                    far[h] = jnp.where(bit, far[h], far[h] * tot_after)
                tot[h] = tot[h] * jnp.where(bit, tot_before, tot_after)
            m *= 2
            level += 1

        for h in range(heads):
            lanes = slice(h * HEAD_DIM, (h + 1) * HEAD_DIM)
            vb = v[h].astype(BF16)
            st = st_ref[h]
            o = jnp.dot(scores[h].astype(BF16), vb, preferred_element_type=F32)
            o = o + jnp.sum(q[h] * kk[h], axis=1, keepdims=True) * v[h]
            o = o + lax.dot_general((q[h] * near[h]).astype(BF16), st.astype(BF16), NT_DIMS,
                                    preferred_element_type=F32)
            st_ref[h] = st * tot[h][0:1] + jnp.dot(v[h].T.astype(BF16), (kk[h] * far[h]).astype(BF16),
                                                   preferred_element_type=F32)
            if finalize:
                o = o + prev_ref[0, rows, lanes]
                ms = jnp.mean(o * o, axis=-1, keepdims=True)
                o = o * lax.rsqrt(ms + EPS) * gn_ref[...]
            o_ref[0, rows, lanes] = o.astype(o_ref.dtype)
        return carry

    lax.fori_loop(0, n, one_chunk, 0)


def hgrn_direction(u, lb, gate_block, *, reverse, prev=None, o_norm=None, rows=1024, chunk=128, heads=4):
    B, S, N = u.shape
    E = N // 5
    W = heads * HEAD_DIM
    HB = E // W
    R = min(rows, S)
    nb = S // R
    finalize = prev is not None
    ridx = (lambda c: nb - 1 - c) if reverse else (lambda c: c)
    col = lambda blk: pl.BlockSpec((1, R, W), lambda b, h, c: (b, ridx(c), blk * HB + h))
    in_specs = [col(0), col(gate_block), col(3), pl.BlockSpec((1, W), lambda b, h, c: (0, h))]
    args = [u, u, u, lb.reshape(1, E)]
    if finalize:
        in_specs += [pl.BlockSpec((1, R, W), lambda b, h, c: (b, ridx(c), h)),
                     pl.BlockSpec((1, HEAD_DIM), lambda b, h, c: (0, 0))]
        args += [prev, o_norm.reshape(1, HEAD_DIM)]
    return pl.pallas_call(
        functools.partial(_hgrn_kernel, reverse=reverse, finalize=finalize, chunk=min(chunk, R)),
        grid=(B, HB, nb),
        in_specs=in_specs,
        out_specs=pl.BlockSpec((1, R, W), lambda b, h, c: (b, ridx(c), h)),
        out_shape=jax.ShapeDtypeStruct((B, S, E), BF16 if finalize else F32),
        scratch_shapes=[pltpu.VMEM((heads, HEAD_DIM, HEAD_DIM), F32)],
        compiler_params=_params(("parallel", "parallel", "arbitrary"), 40),
        name="hgrn_bwd" if reverse else "hgrn_fwd",
    )(*args)


def _attn_proj_kernel(*refs, mode, scale, slab):
    if mode == "q":
        x_ref, g_ref, w_ref, t1_ref, t2_ref, gn_ref, o_ref, h_ref = refs

        @pl.when(pl.program_id(1) == 0)
        def _():
            x = x_ref[...]
            ms = jnp.mean(x * x, axis=-1, keepdims=True)
            h_ref[...] = (x * lax.rsqrt(ms + EPS) * g_ref[...]).astype(h_ref.dtype)
    elif mode == "k":
        h_ref, w_ref, t1_ref, t2_ref, gn_ref, swap_ref, o_ref = refs
    else:
        h_ref, w_ref, o_ref = refs

    half = ROPE_DIM // 2
    project = lambda c0: jnp.dot(h_ref[...], w_ref[:, c0:c0 + slab], preferred_element_type=F32)
    tn = w_ref.shape[1]
    nxt = project(0)
    for c0 in range(0, tn, slab):
        acc, nxt = nxt, (project(c0 + slab) if c0 + slab < tn else None)
        for g0 in range(0, slab, HEAD_DIM):
            y = acc[:, g0:g0 + HEAD_DIM]
            cols = slice(c0 + g0, c0 + g0 + HEAD_DIM)
            if mode == "q":
                yt = y.T
                yt = yt * lax.rsqrt(jnp.mean(yt * yt, axis=0, keepdims=True) + EPS) * (gn_ref[...] * scale)
                lo, hi, cos, sin = yt[:half], yt[half:ROPE_DIM], t1_ref[...], t2_ref[...]
                yt = jnp.concatenate([lo * cos - hi * sin, hi * cos + lo * sin, yt[ROPE_DIM:]], axis=0)
                o_ref[0, cols, :] = yt.astype(o_ref.dtype)
            elif mode == "k":
                yn = y * lax.rsqrt(jnp.mean(y * y, axis=-1, keepdims=True) + EPS) * gn_ref[...]
                partner = jnp.dot(yn.astype(BF16), swap_ref[...], preferred_element_type=F32)
                o_ref[:, cols] = (yn * t1_ref[...] + partner * t2_ref[...]).astype(o_ref.dtype)
            else:
                o_ref[0, cols, :] = y.T.astype(o_ref.dtype)


def attn_projection(x, w, mode, tables=None, head_gain=None, gain=None, *, tm=1024, tn=1024):
    B, S, D = x.shape
    E = w.shape[1]
    tm, tn = min(tm, S), min(tn, E)
    sb = S // tm
    if mode == "k":
        out_specs = [pl.BlockSpec((tm, tn), lambda i, j: (i, j))]
        out_shape = [jax.ShapeDtypeStruct((B * S, E), BF16)]
    else:
        out_specs = [pl.BlockSpec((1, tn, tm), lambda i, j: (i // sb, j, i % sb))]
        out_shape = [jax.ShapeDtypeStruct((B, E, S), BF16)]
    rows = pl.BlockSpec((tm, D), lambda i, j: (i, 0))
    weights = pl.BlockSpec((D, tn), lambda i, j: (0, j))
    half = ROPE_DIM // 2
    if mode == "q":
        in_specs = [rows, pl.BlockSpec((1, D), lambda i, j: (0, 0)), weights,
                    pl.BlockSpec((half, tm), lambda i, j: (0, i)), pl.BlockSpec((half, tm), lambda i, j: (0, i)),
                    pl.BlockSpec((HEAD_DIM, 1), lambda i, j: (0, 0))]
        args = [x.reshape(B * S, D), gain.reshape(1, D), w.astype(BF16), *tables, head_gain.reshape(HEAD_DIM, 1)]
        out_specs.append(rows)
        out_shape.append(jax.ShapeDtypeStruct((B * S, D), BF16))
    elif mode == "k":
        swap = np.zeros((HEAD_DIM, HEAD_DIM), np.float32)
        swap[np.arange(half) + half, np.arange(half)] = 1.0
        swap[np.arange(half), np.arange(half) + half] = 1.0
        tab = pl.BlockSpec((tm, LANES), lambda i, j: (i, 0))
        in_specs = [rows, weights, tab, tab, pl.BlockSpec((1, HEAD_DIM), lambda i, j: (0, 0)),
                    pl.BlockSpec((HEAD_DIM, HEAD_DIM), lambda i, j: (0, 0))]
        args = [x.reshape(B * S, D), w.astype(BF16), *tables, head_gain.reshape(1, HEAD_DIM), jnp.asarray(swap, BF16)]
    else:
        in_specs = [rows, weights]
        args = [x.reshape(B * S, D), w.astype(BF16)]
    out = pl.pallas_call(
        functools.partial(_attn_proj_kernel, mode=mode, scale=math.log2(math.e) / math.sqrt(HEAD_DIM),
                          slab=min(2 * HEAD_DIM, tn)),
        grid=(B * S // tm, E // tn),
        in_specs=in_specs,
        out_specs=out_specs,
        out_shape=out_shape,
        compiler_params=_params(("parallel", "arbitrary"), 48),
        name="attn_proj_" + mode,
    )(*args)
    if mode == "q":
        return out[0], out[1].reshape(B, S, D)
    return out[0].reshape(B, S, E) if mode == "k" else out[0]


def _matmul_kernel(h_ref, w_ref, o_ref):
    o_ref[...] = jnp.dot(h_ref[...], w_ref[...], preferred_element_type=F32)


def matmul(h2d, w, *, tm=1024, tn=2048):
    T, D = h2d.shape
    N = w.shape[1]
    tm, tn = min(tm, T), min(tn, N)
    return pl.pallas_call(
        _matmul_kernel,
        grid=(T // tm, N // tn),
        in_specs=[pl.BlockSpec((tm, D), lambda i, j: (i, 0)), pl.BlockSpec((D, tn), lambda i, j: (0, j))],
        out_specs=pl.BlockSpec((tm, tn), lambda i, j: (i, j)),
        out_shape=jax.ShapeDtypeStruct((T, N), F32),
        compiler_params=_params(("parallel", "parallel"), 48),
        name="matmul",
    )(h2d, w.astype(BF16))


def _attn_kernel(lam_ref, qt_ref, k_ref, vt_ref, sn_ref, o_ref, m_ref, l_ref, acc_ref, s00, s01, s10, s11, mc_ref,
                 *, post_scale, tq, tkc):
    tile = pl.program_id(2)
    n = k_ref.shape[1] // tkc
    s_refs = ((s00, s01), (s10, s11))
    m_ref[...] = jnp.full_like(m_ref, -jnp.inf)
    l_ref[...] = jnp.zeros_like(l_ref)
    acc_ref[...] = jnp.zeros_like(acc_ref)

    def produce(q_tile, c, slot, i):
        sl = slice(i * HEAD_DIM, (i + 1) * HEAD_DIM)
        off = pl.multiple_of(c * tkc, tkc)
        q = qt_ref[0, sl, pl.ds(pl.multiple_of(q_tile * tq, tq), tq)]
        s_new = jnp.dot(k_ref[0, pl.ds(off, tkc), sl], q, preferred_element_type=F32)
        s_refs[slot][i][...] = s_new
        mc_ref[slot, i] = jnp.max(s_new, axis=0, keepdims=True)

    def consume(vt_c, slot, i):
        m_prev = m_ref[i]
        m_new = jnp.maximum(m_prev, mc_ref[slot, i])
        alpha = jnp.exp2(m_prev - m_new)
        p = jnp.exp2(s_refs[slot][i][...] - m_new)
        l_ref[i] = alpha * l_ref[i] + jnp.sum(p, axis=0, keepdims=True)
        acc_ref[i] = alpha * acc_ref[i] + jnp.dot(vt_c, p.astype(BF16), preferred_element_type=F32)
        m_ref[i] = m_new

    def step(c, slot, next_tile, next_chunk):
        vt_c = vt_ref[0, :, pl.ds(pl.multiple_of(c * tkc, tkc), tkc)]
        for i in range(2):
            produce(next_tile, next_chunk, 1 - slot, i)
            consume(vt_c, slot, i)

    @pl.when(tile == 0)
    def _():
        produce(0, 0, 0, 0)
        produce(0, 0, 0, 1)

    def pair(j, carry):
        step(2 * j, 0, tile, 2 * j + 1)
        step(2 * j + 1, 1, tile, 2 * j + 2)
        return carry

    lax.fori_loop(0, n // 2 - 1, pair, 0)
    step(n - 2, 0, tile, n - 1)
    step(n - 1, 1, jnp.minimum(tile + 1, pl.num_programs(2) - 1), 0)
    o = (acc_ref[0] / l_ref[0] - lam_ref[0] * (acc_ref[1] / l_ref[1])).T
    ms = jnp.mean(o * o, axis=-1, keepdims=True)
    o_ref[0] = (o * lax.rsqrt(ms + EPS) * sn_ref[...] * post_scale).astype(o_ref.dtype)


def diff_attention_core(qt, kp, vt, lam, sub_norm, post_scale, *, tq=512, tkc=1024):
    B, S, E = kp.shape
    W = 2 * HEAD_DIM
    H = E // W
    tq, tkc = min(tq, S), min(tkc, S // 2)
    assert S % (2 * tkc) == 0
    return pl.pallas_call(
        functools.partial(_attn_kernel, post_scale=post_scale, tq=tq, tkc=tkc),
        grid=(B, H, S // tq),
        in_specs=[
            pl.BlockSpec(memory_space=pltpu.SMEM),
            pl.BlockSpec((1, W, S), lambda b, h, i: (b, h, 0), pipeline_mode=pl.Buffered(1)),
            pl.BlockSpec((1, S, W), lambda b, h, i: (b, 0, h)),
            pl.BlockSpec((1, W, S), lambda b, h, i: (b, h, 0)),
            pl.BlockSpec((1, W), lambda b, h, i: (0, 0)),
        ],
        out_specs=pl.BlockSpec((1, tq, W), lambda b, h, i: (b, i, h)),
        out_shape=jax.ShapeDtypeStruct((B, S, E), BF16),
        scratch_shapes=[pltpu.VMEM((2, 1, tq), F32), pltpu.VMEM((2, 1, tq), F32), pltpu.VMEM((2, W, tq), F32)]
        + [pltpu.VMEM((tkc, tq), F32)] * 4 + [pltpu.VMEM((2, 2, 1, tq), F32)],
        compiler_params=_params(("parallel", "parallel", "arbitrary"), 56),
        name="diff_attention",
    )(lam.reshape(1), qt, kp, vt, sub_norm.reshape(1, W))


def _dft_tables(n):
    ang = 2.0 * np.pi * ((np.arange(n)[:, None] * np.arange(n)[None, :]) % n) / n
    return jnp.asarray(np.cos(ang), F32), jnp.asarray(np.sin(ang), F32)


def _fnet_a_kernel(x_ref, kc_ref, ks_ref, tc_ref, ts_ref, ar_ref, ai_ref):
    n_hi, digits, ec = x_ref.shape[1], x_ref.shape[2], x_ref.shape[3]
    x = x_ref[0].reshape(n_hi * digits, ec).astype(BF16)
    ar = jnp.dot(kc_ref[...], x, preferred_element_type=F32)
    ai = -jnp.dot(ks_ref[...], x, preferred_element_type=F32)
    tc, ts = tc_ref[0], ts_ref[0]
    ar_ref[0] = (ar * tc + ai * ts).reshape(n_hi, digits, ec)
    ai_ref[0] = (ai * tc - ar * ts).reshape(n_hi, digits, ec)


def _fnet_b_kernel(ar_ref, ai_ref, wr_ref, wi_ref, cc_ref, sc_ref, perm_ref, gw_ref, y_ref, *, norm):
    wr, wi = wr_ref[...], wi_ref[...]
    digits, n_lo = ar_ref.shape[1], ar_ref.shape[2]
    G = gw_ref.shape[2]
    br, bi = [], []
    for j in range(digits):
        a = jnp.concatenate([ar_ref[0, j], ai_ref[0, j]], axis=0).astype(BF16)
        br.append(jnp.dot(wr, a, preferred_element_type=F32).astype(BF16))
        bi.append(jnp.dot(wi, a, preferred_element_type=F32).astype(BF16))
    br, bi = jnp.concatenate(br, axis=0), jnp.concatenate(bi, axis=0)
    for g in range(gw_ref.shape[0]):
        sl = slice(g * G, (g + 1) * G)
        f = (jnp.dot(br[:, sl], cc_ref[...], preferred_element_type=F32)
             + jnp.dot(bi[:, sl], sc_ref[...], preferred_element_type=F32)) * norm
        f = jnp.dot(perm_ref[...], f.astype(BF16), preferred_element_type=F32).astype(BF16)
        y_ref[0, :, :, sl] = jnp.dot(f, gw_ref[g], preferred_element_type=F32).reshape(n_lo, digits, G)


def fourier_core(u, group_w, *, n_lo=128, e_chunk=512):
    B, S, N = u.shape
    E = N // 2
    G = E // FN_GROUPS
    n_hi = S // n_lo
    Ec = min(e_chunk, E)
    d = SUBLANES
    assert n_hi % d == 0 and n_lo % d == 0
    c_hi, s_hi = _dft_tables(n_hi)
    c_lo, s_lo = _dft_tables(n_lo)
    c_ch, s_ch = _dft_tables(G)
    eye = jnp.eye(d, dtype=F32)
    ang = 2.0 * np.pi * ((np.arange(n_lo)[:, None] * np.arange(n_hi)[None, :]) % S) / S
    tw = lambda t: jnp.asarray(t, F32).reshape(n_lo // d, d, n_hi).transpose(0, 2, 1).reshape(n_lo // d, n_hi * d, 1)
    perm = np.zeros((n_lo, d, d, n_lo), np.float32)
    perm[np.arange(n_lo)[:, None], np.arange(d)[None, :], np.arange(d)[None, :], np.arange(n_lo)[:, None]] = 1.0
    mat = lambda n: pl.BlockSpec((n, n), lambda b, j, e: (0, 0))
    tws = pl.BlockSpec((1, n_hi * d, 1), lambda b, j, e: (j, 0, 0))
    a_blk = pl.BlockSpec((1, n_hi, d, Ec), lambda b, j, e: (b, 0, j, e))
    ar, ai = pl.pallas_call(
        _fnet_a_kernel,
        grid=(B, n_lo // d, E // Ec),
        in_specs=[a_blk, mat(n_hi * d), mat(n_hi * d), tws, tws],
        out_specs=[a_blk, a_blk],
        out_shape=[jax.ShapeDtypeStruct((B, n_hi, n_lo, E), F32)] * 2,
        compiler_params=_params(("parallel", "parallel", "parallel"), 40),
        name="fnet_stage_a",
    )(u.reshape(B, n_hi, n_lo, N), jnp.kron(c_hi, eye).astype(BF16), jnp.kron(s_hi, eye).astype(BF16),
      tw(np.cos(ang)), tw(np.sin(ang)))

    a_in = pl.BlockSpec((1, d, n_lo, Ec), lambda b, j, e: (b, j, 0, e))
    wide = pl.BlockSpec((n_lo, 2 * n_lo), lambda b, j, e: (0, 0))
    y = pl.pallas_call(
        functools.partial(_fnet_b_kernel, norm=1.0 / math.sqrt(S * G)),
        grid=(B, n_hi // d, E // Ec),
        in_specs=[a_in, a_in, wide, wide, mat(G), mat(G), mat(n_lo * d),
                  pl.BlockSpec((Ec // G, G, G), lambda b, j, e: (e, 0, 0))],
        out_specs=pl.BlockSpec((1, n_lo, d, Ec), lambda b, j, e: (b, 0, j, e)),
        out_shape=jax.ShapeDtypeStruct((B, n_lo, n_hi, E), F32),
        compiler_params=_params(("parallel", "parallel", "parallel"), 40),
        name="fnet_stage_b",
    )(ar, ai, jnp.concatenate([c_lo, s_lo], axis=1).astype(BF16), jnp.concatenate([-s_lo, c_lo], axis=1).astype(BF16),
      c_ch.astype(BF16), s_ch.astype(BF16),
      jnp.asarray(perm.reshape(n_lo * d, d * n_lo), BF16), group_w.astype(BF16))
    return y.reshape(B, S, E)


def conv_layer(x, norm, w_in, dw, dw_b, ln_g, ln_b, w_out):
    B, S, D = x.shape
    E = w_out.shape[0]
    u = norm_matmul(x.reshape(B * S, D), norm, w_in)
    return conv_mixer(u.reshape(B, S, 3 * E), x.reshape(B * S, D), dw, dw_b, ln_g, ln_b, w_out).reshape(B, S, D)


def hgrn_layer(x, norm, w_in, lb_fwd, lb_bwd, o_norm, w_out):
    B, S, D = x.shape
    E = w_out.shape[0]
    u = norm_matmul(x.reshape(B * S, D), norm, w_in)
    u3 = u.reshape(B, S, 5 * E)
    o_fwd = hgrn_direction(u3, lb_fwd, 1, reverse=False)
    o = hgrn_direction(u3, lb_bwd, 2, reverse=True, prev=o_fwd, o_norm=o_norm)
    return gate_out(o.reshape(B * S, E), u, 4, x.reshape(B * S, D), w_out).reshape(B, S, D)


def rope_tables(positions):
    half = ROPE_DIM // 2
    inv = 1.0 / (ROPE_THETA ** (jnp.arange(0, ROPE_DIM, 2, dtype=F32) / ROPE_DIM))
    ang = positions.astype(F32).reshape(-1, 1) * inv
    cos, sin = jnp.cos(ang), jnp.sin(ang)
    T = ang.shape[0]
    rest = LANES - ROPE_DIM
    keep = jnp.concatenate([cos, cos, jnp.ones((T, rest), F32)], axis=-1)
    partner = jnp.concatenate([-sin, sin, jnp.zeros((T, rest), F32)], axis=-1)
    return (cos.T, sin.T), (keep, partner)


def diff_layer(x, rope, norm, w_in, q_norm, k_norm, lam_q1, lam_k1, lam_q2, lam_k2, sub_norm, w_out, lam_init):
    B, S, D = x.shape
    E = w_out.shape[0]
    wq, wk, wv, wz = (w_in[:, i * E:(i + 1) * E] for i in range(4))
    qt, h = attn_projection(x, wq, "q", rope[0], q_norm, norm)
    kp = attn_projection(h, wk, "k", rope[1], k_norm)
    vt = attn_projection(h, wv, "v")
    z = matmul(h.reshape(B * S, D), wz)
    lam = jnp.exp(jnp.sum(lam_q1 * lam_k1)) - jnp.exp(jnp.sum(lam_q2 * lam_k2)) + lam_init
    o = diff_attention_core(qt, kp, vt, lam, sub_norm, 1.0 - lam_init)
    return gate_out(o.reshape(B * S, E), z, 0, x.reshape(B * S, D), w_out).reshape(B, S, D)


def fnet_layer(x, norm, w_in, group_w, w_out):
    B, S, D = x.shape
    E = w_out.shape[0]
    u = norm_matmul(x.reshape(B * S, D), norm, w_in)
    y = fourier_core(u.reshape(B, S, 2 * E), group_w)
    return gate_out(y.reshape(B * S, E), u, 1, x.reshape(B * S, D), w_out).reshape(B, S, D)


def hgrn_lower_bounds(table):
    lb = jnp.cumsum(jax.nn.softmax(table.astype(F32), axis=0), axis=0)
    return lb - lb[0:1]


def kernel(x, positions, conv_norm, conv_w_in, conv_dw, conv_dw_b, conv_ln_g, conv_ln_b, conv_w_out, hgrn_norm, hgrn_w_in, hgrn_lb_fwd, hgrn_lb_bwd, hgrn_o_norm, hgrn_w_out, diff_norm, diff_w_in, diff_q_norm, diff_k_norm, diff_lam_q1, diff_lam_k1, diff_lam_q2, diff_lam_k2, diff_sub_norm, diff_w_out, fnet_norm, fnet_w_in, fnet_group_w, fnet_w_out):
    depth = hgrn_lb_fwd.shape[0]
    n_mixers = 4
    rope = rope_tables(positions)
    lb_fwd = hgrn_lower_bounds(hgrn_lb_fwd)
    lb_bwd = hgrn_lower_bounds(hgrn_lb_bwd)
    for layer in range(depth):
        m, j = layer % n_mixers, layer // n_mixers
        if m == 0:
            x = conv_layer(x, conv_norm[j], conv_w_in[j], conv_dw[j], conv_dw_b[j], conv_ln_g[j], conv_ln_b[j],
                           conv_w_out[j])
        elif m == 1:
            x = hgrn_layer(x, hgrn_norm[j], hgrn_w_in[j], lb_fwd[layer], lb_bwd[layer], hgrn_o_norm[j],
                           hgrn_w_out[j])
        elif m == 2:
            lam_init = 0.8 - 0.6 * math.exp(-0.3 * layer)
            x = diff_layer(x, rope, diff_norm[j], diff_w_in[j], diff_q_norm[j], diff_k_norm[j], diff_lam_q1[j],
                           diff_lam_k1[j], diff_lam_q2[j], diff_lam_k2[j], diff_sub_norm[j], diff_w_out[j], lam_init)
        else:
            x = fnet_layer(x, fnet_norm[j], fnet_w_in[j], fnet_group_w[j], fnet_w_out[j])
    return x
```

```python
import functools
import math

import numpy as np
import jax
import jax.numpy as jnp
from jax import lax
from jax.experimental import pallas as pl
from jax.experimental.pallas import tpu as pltpu

EPS = 1e-6
LANES = 128
SUBLANES = 8
HEAD_DIM = 128
CONV_HALO = 16
ROPE_DIM = 32
ROPE_THETA = 500000.0
FN_GROUPS = 8
MIB = 1 << 20

F32 = jnp.float32
BF16 = jnp.bfloat16
NT_DIMS = (((1,), (1,)), ((), ()))


def _params(semantics, vmem_mib):
    return pltpu.CompilerParams(dimension_semantics=semantics, vmem_limit_bytes=vmem_mib * MIB)


def _silu(z):
    return z * jax.nn.sigmoid(z)


def _norm_matmul_kernel(x_ref, g_ref, w_ref, o_ref, h_ref):
    @pl.when(pl.program_id(1) == 0)
    def _():
        x = x_ref[...]
        ms = jnp.mean(x * x, axis=-1, keepdims=True)
        h_ref[...] = (x * lax.rsqrt(ms + EPS) * g_ref[...]).astype(h_ref.dtype)

    o_ref[...] = jnp.dot(h_ref[...], w_ref[...], preferred_element_type=F32).astype(o_ref.dtype)


def norm_matmul(x2d, gain, w, *, tm=1024, tn=2048):
    T, D = x2d.shape
    N = w.shape[1]
    tm, tn = min(tm, T), min(tn, N)
    return pl.pallas_call(
        _norm_matmul_kernel,
        grid=(T // tm, N // tn),
        in_specs=[
            pl.BlockSpec((tm, D), lambda i, j: (i, 0)),
            pl.BlockSpec((1, D), lambda i, j: (0, 0)),
            pl.BlockSpec((D, tn), lambda i, j: (0, j)),
        ],
        out_specs=pl.BlockSpec((tm, tn), lambda i, j: (i, j)),
        out_shape=jax.ShapeDtypeStruct((T, N), F32),
        scratch_shapes=[pltpu.VMEM((tm, D), BF16)],
        compiler_params=_params(("parallel", "arbitrary"), 58),
        name="norm_matmul",
    )(x2d, gain.reshape(1, D), w.astype(BF16))


def _gate_out_kernel(o_ref, z_ref, x_ref, w_ref, out_ref):
    g = (o_ref[...] * _silu(z_ref[...])).astype(BF16)
    out_ref[...] = x_ref[...] + jnp.dot(g, w_ref[...], preferred_element_type=F32)


def gate_out(o2d, u2d, z_block, x2d, w_out, *, tm=512):
    T, E = o2d.shape
    D = w_out.shape[1]
    tm = min(tm, T)
    return pl.pallas_call(
        _gate_out_kernel,
        grid=(T // tm,),
        in_specs=[
            pl.BlockSpec((tm, E), lambda i: (i, 0)),
            pl.BlockSpec((tm, E), lambda i: (i, z_block)),
            pl.BlockSpec((tm, D), lambda i: (i, 0)),
            pl.BlockSpec((E, D), lambda i: (0, 0), pipeline_mode=pl.Buffered(1)),
        ],
        out_specs=pl.BlockSpec((tm, D), lambda i: (i, 0)),
        out_shape=jax.ShapeDtypeStruct((T, D), F32),
        compiler_params=_params(("parallel",), 48),
        name="gate_out",
    )(o2d, u2d, x2d, w_out.astype(BF16))


def _conv_kernel(am_ref, ap_ref, an_ref, bm_ref, bp_ref, bn_ref, z_ref, x_ref, dw_ref, dwb_ref, g_ref, beta_ref,
                 w_ref, o_ref, vbuf, shifted, cbuf, *, taps, row_block, col_block):
    s = pl.program_id(1)
    ts, E = am_ref.shape[1], am_ref.shape[2]
    glu = lambda a, b: a * jax.nn.sigmoid(b)
    vbuf[0:CONV_HALO, :] = jnp.where(s > 0, glu(ap_ref[0], bp_ref[0]), 0.0)
    vbuf[CONV_HALO:CONV_HALO + ts, :] = glu(am_ref[0], bm_ref[0])
    vbuf[CONV_HALO + ts:, :] = jnp.where(s < pl.num_programs(1) - 1, glu(an_ref[0], bn_ref[0]), 0.0)
    span = shifted.shape[1]
    for r in range(1, SUBLANES):
        shifted[r - 1] = vbuf[r:r + span, :]
    first = CONV_HALO - taps // 2
    for r0 in range(0, ts, row_block):
        for c0 in range(0, E, col_block):
            cols = slice(c0, c0 + col_block)
            acc = jnp.zeros((row_block, col_block), F32)
            for k in range(taps):
                phase, base = (first + k) % SUBLANES, r0 + (first + k) // SUBLANES * SUBLANES
                win = (vbuf[base:base + row_block, cols] if phase == 0
                       else shifted[phase - 1, base:base + row_block, cols])
                acc = acc + dw_ref[k:k + 1, cols] * win
            cbuf[r0:r0 + row_block, cols] = acc + dwb_ref[:, cols]
    c = cbuf[...]
    mu = jnp.mean(c, axis=-1, keepdims=True)
    cc = c - mu
    var = jnp.mean(cc * cc, axis=-1, keepdims=True)
    y = cc * lax.rsqrt(var + EPS) * g_ref[...] + beta_ref[...]
    gated = (_silu(y) * _silu(z_ref[0])).astype(BF16)
    o_ref[...] = x_ref[...] + jnp.dot(gated, w_ref[...], preferred_element_type=F32)


def conv_mixer(u, x2d, dw, dw_b, ln_g, ln_b, w_out, *, ts=256):
    B, S, N = u.shape
    E = N // 3
    D = w_out.shape[1]
    taps = dw.shape[0]
    ts = min(ts, S)
    hb = ts // CONV_HALO
    st = S // ts
    last_halo = S // CONV_HALO - 1
    main = lambda col: pl.BlockSpec((1, ts, E), lambda b, s: (b, s, col))
    prev = lambda col: pl.BlockSpec((1, CONV_HALO, E), lambda b, s: (b, jnp.maximum(s * hb - 1, 0), col))
    nxt = lambda col: pl.BlockSpec((1, CONV_HALO, E), lambda b, s: (b, jnp.minimum((s + 1) * hb, last_halo), col))
    vec = lambda rows: pl.BlockSpec((rows, E), lambda b, s: (0, 0))
    rows = pl.BlockSpec((ts, D), lambda b, s: (b * st + s, 0))
    return pl.pallas_call(
        functools.partial(_conv_kernel, taps=taps, row_block=32, col_block=512),
        grid=(B, st),
        in_specs=[main(0), prev(0), nxt(0), main(1), prev(1), nxt(1), main(2), rows, vec(taps), vec(1), vec(1), vec(1),
                  pl.BlockSpec((E, D), lambda b, s: (0, 0), pipeline_mode=pl.Buffered(1))],
        out_specs=rows,
        out_shape=jax.ShapeDtypeStruct((B * S, D), F32),
        scratch_shapes=[pltpu.VMEM((ts + 2 * CONV_HALO, E), F32),
                        pltpu.VMEM((SUBLANES - 1, ts + 2 * CONV_HALO - SUBLANES, E), F32),
                        pltpu.VMEM((ts, E), F32)],
        compiler_params=_params(("parallel", "parallel"), 58),
        name="conv_mixer",
    )(u, u, u, u, u, u, u, x2d, dw, dw_b.reshape(1, E), ln_g.reshape(1, E), ln_b.reshape(1, E), w_out.astype(BF16))


def _hgrn_kernel(*refs, reverse, finalize, chunk):
    if finalize:
        q_ref, a_ref, v_ref, lb_ref, prev_ref, gn_ref, o_ref, st_ref = refs
    else:
        q_ref, a_ref, v_ref, lb_ref, o_ref, st_ref = refs
    C = chunk
    n = q_ref.shape[1] // C
    heads = q_ref.shape[2] // HEAD_DIM

    @pl.when(pl.program_id(2) == 0)
    def _():
        st_ref[...] = jnp.zeros_like(st_ref)

    row = lax.broadcasted_iota(jnp.int32, (C, HEAD_DIM), 0)
    pair_xor = lax.broadcasted_iota(jnp.int32, (C, C), 0) ^ lax.broadcasted_iota(jnp.int32, (C, C), 1)
    tt, ss = lax.broadcasted_iota(jnp.int32, (C, C), 0), lax.broadcasted_iota(jnp.int32, (C, C), 1)
    pair_level = jnp.where((tt < ss) if reverse else (tt > ss),
                           (pltpu.bitcast(pair_xor.astype(F32), jnp.int32) >> 23) - 127, -1)

    def one_chunk(j, carry):
        r0 = pl.multiple_of((n - 1 - j if reverse else j) * C, C)
        rows = pl.ds(r0, C)
        q, kk, v, near, far, tot, scores = [], [], [], [], [], [], []
        for h in range(heads):
            lanes = slice(h * HEAD_DIM, (h + 1) * HEAD_DIM)
            a, lb = a_ref[0, rows, lanes], lb_ref[:, lanes]
            e = jnp.exp(-jnp.abs(a))
            r = 1.0 / (1.0 + e)
            er = e * r
            pos = a >= 0
            f = lb + (1.0 - lb) * jnp.where(pos, r, er)
            q.append(q_ref[0, rows, lanes])
            v.append(v_ref[0, rows, lanes])
            kk.append((1.0 - lb) * jnp.where(pos, er, r))
            near.append(f)
            far.append(jnp.ones_like(f))
            tot.append(f)
            scores.append(jnp.zeros((C, C), F32))

        m, level = 1, 0
        while m < C:
            bit = (row & m) != 0
            for h in range(heads):
                qm = (q[h] * near[h]).astype(BF16)
                km = (kk[h] * far[h]).astype(BF16)
                sc = lax.dot_general(qm, km, NT_DIMS, preferred_element_type=F32)
                scores[h] = jnp.where(pair_level == level, sc, scores[h])
                tot_before = pltpu.roll(tot[h], m, 0)
                tot_after = pltpu.roll(tot[h], C - m, 0)
                if reverse:
                    near[h] = jnp.where(bit, near[h], near[h] * tot_after)
                    far[h] = jnp.where(bit, far[h] * tot_before, far[h])
                else:
                    near[h] = jnp.where(bit, near[h] * tot_before, near[h])
                    far[h] = jnp.where(bit, far[h], far[h] * tot_after)
                tot[h] = tot[h] * jnp.where(bit, tot_before, tot_after)
            m *= 2
            level += 1

        for h in range(heads):
            lanes = slice(h * HEAD_DIM, (h + 1) * HEAD_DIM)
            vb = v[h].astype(BF16)
            st = st_ref[h]
            o = jnp.dot(scores[h].astype(BF16), vb, preferred_element_type=F32)
            o = o + jnp.sum(q[h] * kk[h], axis=1, keepdims=True) * v[h]
            o = o + lax.dot_general((q[h] * near[h]).astype(BF16), st.astype(BF16), NT_DIMS,
                                    preferred_element_type=F32)
            st_ref[h] = st * tot[h][0:1] + jnp.dot(v[h].T.astype(BF16), (kk[h] * far[h]).astype(BF16),
                                                   preferred_element_type=F32)
            if finalize:
                o = o + prev_ref[0, rows, lanes]
                ms = jnp.mean(o * o, axis=-1, keepdims=True)
                o = o * lax.rsqrt(ms + EPS) * gn_ref[...]
            o_ref[0, rows, lanes] = o.astype(o_ref.dtype)
        return carry

    lax.fori_loop(0, n, one_chunk, 0)


def hgrn_direction(u, lb, gate_block, *, reverse, prev=None, o_norm=None, rows=1024, chunk=128, heads=8):
    B, S, N = u.shape
    E = N // 5
    W = heads * HEAD_DIM
    HB = E // W
    R = min(rows, S)
    nb = S // R
    finalize = prev is not None
    ridx = (lambda c: nb - 1 - c) if reverse else (lambda c: c)
    col = lambda blk: pl.BlockSpec((1, R, W), lambda b, h, c: (b, ridx(c), blk * HB + h))
    in_specs = [col(0), col(gate_block), col(3), pl.BlockSpec((1, W), lambda b, h, c: (0, h))]
    args = [u, u, u, lb.reshape(1, E)]
    if finalize:
        in_specs += [pl.BlockSpec((1, R, W), lambda b, h, c: (b, ridx(c), h)),
                     pl.BlockSpec((1, HEAD_DIM), lambda b, h, c: (0, 0))]
        args += [prev, o_norm.reshape(1, HEAD_DIM)]
    return pl.pallas_call(
        functools.partial(_hgrn_kernel, reverse=reverse, finalize=finalize, chunk=min(chunk, R)),
        grid=(B, HB, nb),
        in_specs=in_specs,
        out_specs=pl.BlockSpec((1, R, W), lambda b, h, c: (b, ridx(c), h)),
        out_shape=jax.ShapeDtypeStruct((B, S, E), BF16 if finalize else F32),
        scratch_shapes=[pltpu.VMEM((heads, HEAD_DIM, HEAD_DIM), F32)],
        compiler_params=_params(("parallel", "parallel", "arbitrary"), 56),
        name="hgrn_bwd" if reverse else "hgrn_fwd",
    )(*args)


def _attn_proj_kernel(*refs, mode, scale, slab):
    if mode == "q":
        x_ref, g_ref, w_ref, t1_ref, t2_ref, gn_ref, o_ref, h_ref = refs

        @pl.when(pl.program_id(1) == 0)
        def _():
            x = x_ref[...]
            ms = jnp.mean(x * x, axis=-1, keepdims=True)
            h_ref[...] = (x * lax.rsqrt(ms + EPS) * g_ref[...]).astype(h_ref.dtype)
    elif mode == "k":
        h_ref, w_ref, t1_ref, t2_ref, gn_ref, swap_ref, o_ref = refs
    else:
        h_ref, w_ref, o_ref = refs

    half = ROPE_DIM // 2
    project = lambda c0: jnp.dot(h_ref[...], w_ref[:, c0:c0 + slab], preferred_element_type=F32)
    tn = w_ref.shape[1]
    nxt = project(0)
    for c0 in range(0, tn, slab):
        acc, nxt = nxt, (project(c0 + slab) if c0 + slab < tn else None)
        for g0 in range(0, slab, HEAD_DIM):
            y = acc[:, g0:g0 + HEAD_DIM]
            cols = slice(c0 + g0, c0 + g0 + HEAD_DIM)
            if mode == "q":
                yt = y.T
                yt = yt * lax.rsqrt(jnp.mean(yt * yt, axis=0, keepdims=True) + EPS) * (gn_ref[...] * scale)
                lo, hi, cos, sin = yt[:half], yt[half:ROPE_DIM], t1_ref[...], t2_ref[...]
                yt = jnp.concatenate([lo * cos - hi * sin, hi * cos + lo * sin, yt[ROPE_DIM:]], axis=0)
                o_ref[0, cols, :] = yt.astype(o_ref.dtype)
            elif mode == "k":
                yn = y * lax.rsqrt(jnp.mean(y * y, axis=-1, keepdims=True) + EPS) * gn_ref[...]
                partner = jnp.dot(yn.astype(BF16), swap_ref[...], preferred_element_type=F32)
                o_ref[:, cols] = (yn * t1_ref[...] + partner * t2_ref[...]).astype(o_ref.dtype)
            else:
                o_ref[0, cols, :] = y.T.astype(o_ref.dtype)


def attn_projection(x, w, mode, tables=None, head_gain=None, gain=None, *, tm=1024, tn=2048):
    B, S, D = x.shape
    E = w.shape[1]
    tm, tn = min(tm, S), min(tn, E)
    sb = S // tm
    if mode == "k":
        out_specs = [pl.BlockSpec((tm, tn), lambda i, j: (i, j))]
        out_shape = [jax.ShapeDtypeStruct((B * S, E), BF16)]
    else:
        out_specs = [pl.BlockSpec((1, tn, tm), lambda i, j: (i // sb, j, i % sb))]
        out_shape = [jax.ShapeDtypeStruct((B, E, S), BF16)]
    rows = pl.BlockSpec((tm, D), lambda i, j: (i, 0))
    weights = pl.BlockSpec((D, tn), lambda i, j: (0, j))
    half = ROPE_DIM // 2
    if mode == "q":
        in_specs = [rows, pl.BlockSpec((1, D), lambda i, j: (0, 0)), weights,
                    pl.BlockSpec((half, tm), lambda i, j: (0, i)), pl.BlockSpec((half, tm), lambda i, j: (0, i)),
                    pl.BlockSpec((HEAD_DIM, 1), lambda i, j: (0, 0))]
        args = [x.reshape(B * S, D), gain.reshape(1, D), w.astype(BF16), *tables, head_gain.reshape(HEAD_DIM, 1)]
        out_specs.append(rows)
        out_shape.append(jax.ShapeDtypeStruct((B * S, D), BF16))
    elif mode == "k":
        swap = np.zeros((HEAD_DIM, HEAD_DIM), np.float32)
        swap[np.arange(half) + half, np.arange(half)] = 1.0
        swap[np.arange(half), np.arange(half) + half] = 1.0
        tab = pl.BlockSpec((tm, LANES), lambda i, j: (i, 0))
        in_specs = [rows, weights, tab, tab, pl.BlockSpec((1, HEAD_DIM), lambda i, j: (0, 0)),
                    pl.BlockSpec((HEAD_DIM, HEAD_DIM), lambda i, j: (0, 0))]
        args = [x.reshape(B * S, D), w.astype(BF16), *tables, head_gain.reshape(1, HEAD_DIM), jnp.asarray(swap, BF16)]
    else:
        in_specs = [rows, weights]
        args = [x.reshape(B * S, D), w.astype(BF16)]
    out = pl.pallas_call(
        functools.partial(_attn_proj_kernel, mode=mode, scale=math.log2(math.e) / math.sqrt(HEAD_DIM),
                          slab=min(2 * HEAD_DIM, tn)),
        grid=(B * S // tm, E // tn),
        in_specs=in_specs,
        out_specs=out_specs,
        out_shape=out_shape,
        compiler_params=_params(("parallel", "arbitrary"), 58),
        name="attn_proj_" + mode,
    )(*args)
    if mode == "q":
        return out[0], out[1].reshape(B, S, D)
    return out[0].reshape(B, S, E) if mode == "k" else out[0]


def _matmul_kernel(h_ref, w_ref, o_ref):
    o_ref[...] = jnp.dot(h_ref[...], w_ref[...], preferred_element_type=F32)


def matmul(h2d, w, *, tm=1024, tn=2048):
    T, D = h2d.shape
    N = w.shape[1]
    tm, tn = min(tm, T), min(tn, N)
    return pl.pallas_call(
        _matmul_kernel,
        grid=(T // tm, N // tn),
        in_specs=[pl.BlockSpec((tm, D), lambda i, j: (i, 0)), pl.BlockSpec((D, tn), lambda i, j: (0, j))],
        out_specs=pl.BlockSpec((tm, tn), lambda i, j: (i, j)),
        out_shape=jax.ShapeDtypeStruct((T, N), F32),
        compiler_params=_params(("parallel", "parallel"), 48),
        name="matmul",
    )(h2d, w.astype(BF16))


def _attn_kernel(lam_ref, qt_ref, k_ref, vt_ref, sn_ref, o_ref, m_ref, l_ref, acc_ref, s00, s01, s10, s11, mc_ref,
                 *, post_scale, tq, tkc):
    tile = pl.program_id(2)
    n = k_ref.shape[1] // tkc
    s_refs = ((s00, s01), (s10, s11))
    m_ref[...] = jnp.full_like(m_ref, -jnp.inf)
    l_ref[...] = jnp.zeros_like(l_ref)
    acc_ref[...] = jnp.zeros_like(acc_ref)

    def produce(q_tile, c, slot, i):
        sl = slice(i * HEAD_DIM, (i + 1) * HEAD_DIM)
        off = pl.multiple_of(c * tkc, tkc)
        q = qt_ref[0, sl, pl.ds(pl.multiple_of(q_tile * tq, tq), tq)]
        s_new = jnp.dot(k_ref[0, pl.ds(off, tkc), sl], q, preferred_element_type=F32)
        s_refs[slot][i][...] = s_new
        mc_ref[slot, i] = jnp.max(s_new, axis=0, keepdims=True)

    def consume(vt_c, slot, i):
        m_prev = m_ref[i]
        m_new = jnp.maximum(m_prev, mc_ref[slot, i])
        alpha = jnp.exp2(m_prev - m_new)
        p = jnp.exp2(s_refs[slot][i][...] - m_new)
        l_ref[i] = alpha * l_ref[i] + jnp.sum(p, axis=0, keepdims=True)
        acc_ref[i] = alpha * acc_ref[i] + jnp.dot(vt_c, p.astype(BF16), preferred_element_type=F32)
        m_ref[i] = m_new

    def step(c, slot, next_tile, next_chunk):
        vt_c = vt_ref[0, :, pl.ds(pl.multiple_of(c * tkc, tkc), tkc)]
        for i in range(2):
            produce(next_tile, next_chunk, 1 - slot, i)
            consume(vt_c, slot, i)

    @pl.when(tile == 0)
    def _():
        produce(0, 0, 0, 0)
        produce(0, 0, 0, 1)

    def pair(j, carry):
        step(2 * j, 0, tile, 2 * j + 1)
        step(2 * j + 1, 1, tile, 2 * j + 2)
        return carry

    lax.fori_loop(0, n // 2 - 1, pair, 0)
    step(n - 2, 0, tile, n - 1)
    step(n - 1, 1, jnp.minimum(tile + 1, pl.num_programs(2) - 1), 0)
    o = (acc_ref[0] / l_ref[0] - lam_ref[0] * (acc_ref[1] / l_ref[1])).T
    ms = jnp.mean(o * o, axis=-1, keepdims=True)
    o_ref[0] = (o * lax.rsqrt(ms + EPS) * sn_ref[...] * post_scale).astype(o_ref.dtype)


def diff_attention_core(qt, kp, vt, lam, sub_norm, post_scale, *, tq=512, tkc=1024):
    B, S, E = kp.shape
    W = 2 * HEAD_DIM
    H = E // W
    tq, tkc = min(tq, S), min(tkc, S // 2)
    assert S % (2 * tkc) == 0
    return pl.pallas_call(
        functools.partial(_attn_kernel, post_scale=post_scale, tq=tq, tkc=tkc),
        grid=(B, H, S // tq),
        in_specs=[
            pl.BlockSpec(memory_space=pltpu.SMEM),
            pl.BlockSpec((1, W, S), lambda b, h, i: (b, h, 0), pipeline_mode=pl.Buffered(1)),
            pl.BlockSpec((1, S, W), lambda b, h, i: (b, 0, h)),
            pl.BlockSpec((1, W, S), lambda b, h, i: (b, h, 0)),
            pl.BlockSpec((1, W), lambda b, h, i: (0, 0)),
        ],
        out_specs=pl.BlockSpec((1, tq, W), lambda b, h, i: (b, i, h)),
        out_shape=jax.ShapeDtypeStruct((B, S, E), BF16),
        scratch_shapes=[pltpu.VMEM((2, 1, tq), F32), pltpu.VMEM((2, 1, tq), F32), pltpu.VMEM((2, W, tq), F32)]
        + [pltpu.VMEM((tkc, tq), F32)] * 4 + [pltpu.VMEM((2, 2, 1, tq), F32)],
        compiler_params=_params(("parallel", "parallel", "arbitrary"), 56),
        name="diff_attention",
    )(lam.reshape(1), qt, kp, vt, sub_norm.reshape(1, W))


def _dft_tables(n):
    ang = 2.0 * np.pi * ((np.arange(n)[:, None] * np.arange(n)[None, :]) % n) / n
    return jnp.asarray(np.cos(ang), F32), jnp.asarray(np.sin(ang), F32)


def _fnet_a_kernel(x_ref, kc_ref, ks_ref, tc_ref, ts_ref, ar_ref, ai_ref):
    n_hi, digits, ec = x_ref.shape[1], x_ref.shape[2], x_ref.shape[3]
    x = x_ref[0].reshape(n_hi * digits, ec).astype(BF16)
    ar = jnp.dot(kc_ref[...], x, preferred_element_type=F32)
    ai = -jnp.dot(ks_ref[...], x, preferred_element_type=F32)
    tc, ts = tc_ref[0], ts_ref[0]
    ar_ref[0] = (ar * tc + ai * ts).reshape(n_hi, digits, ec)
    ai_ref[0] = (ai * tc - ar * ts).reshape(n_hi, digits, ec)


def _fnet_b_kernel(ar_ref, ai_ref, wr_ref, wi_ref, cc_ref, sc_ref, perm_ref, gw_ref, y_ref, *, norm):
    wr, wi = wr_ref[...], wi_ref[...]
    digits, n_lo = ar_ref.shape[1], ar_ref.shape[2]
    G = gw_ref.shape[2]
    br, bi = [], []
    for j in range(digits):
        a = jnp.concatenate([ar_ref[0, j], ai_ref[0, j]], axis=0).astype(BF16)
        br.append(jnp.dot(wr, a, preferred_element_type=F32).astype(BF16))
        bi.append(jnp.dot(wi, a, preferred_element_type=F32).astype(BF16))
    br, bi = jnp.concatenate(br, axis=0), jnp.concatenate(bi, axis=0)
    for g in range(gw_ref.shape[0]):
        sl = slice(g * G, (g + 1) * G)
        f = (jnp.dot(br[:, sl], cc_ref[...], preferred_element_type=F32)
             + jnp.dot(bi[:, sl], sc_ref[...], preferred_element_type=F32)) * norm
        f = jnp.dot(perm_ref[...], f.astype(BF16), preferred_element_type=F32).astype(BF16)
        y_ref[0, :, :, sl] = jnp.dot(f, gw_ref[g], preferred_element_type=F32).reshape(n_lo, digits, G)


def fourier_core(u, group_w, *, n_lo=128, e_chunk=1024):
    B, S, N = u.shape
    E = N // 2
    G = E // FN_GROUPS
    n_hi = S // n_lo
    Ec = min(e_chunk, E)
    d = SUBLANES
    assert n_hi % d == 0 and n_lo % d == 0
    c_hi, s_hi = _dft_tables(n_hi)
    c_lo, s_lo = _dft_tables(n_lo)
    c_ch, s_ch = _dft_tables(G)
    eye = jnp.eye(d, dtype=F32)
    ang = 2.0 * np.pi * ((np.arange(n_lo)[:, None] * np.arange(n_hi)[None, :]) % S) / S
    tw = lambda t: jnp.asarray(t, F32).reshape(n_lo // d, d, n_hi).transpose(0, 2, 1).reshape(n_lo // d, n_hi * d, 1)
    perm = np.zeros((n_lo, d, d, n_lo), np.float32)
    perm[np.arange(n_lo)[:, None], np.arange(d)[None, :], np.arange(d)[None, :], np.arange(n_lo)[:, None]] = 1.0
    mat = lambda n: pl.BlockSpec((n, n), lambda b, j, e: (0, 0))
    tws = pl.BlockSpec((1, n_hi * d, 1), lambda b, j, e: (j, 0, 0))
    a_blk = pl.BlockSpec((1, n_hi, d, Ec), lambda b, j, e: (b, 0, j, e))
    ar, ai = pl.pallas_call(
        _fnet_a_kernel,
        grid=(B, n_lo // d, E // Ec),
        in_specs=[a_blk, mat(n_hi * d), mat(n_hi * d), tws, tws],
        out_specs=[a_blk, a_blk],
        out_shape=[jax.ShapeDtypeStruct((B, n_hi, n_lo, E), F32)] * 2,
        compiler_params=_params(("parallel", "parallel", "parallel"), 56),
        name="fnet_stage_a",
    )(u.reshape(B, n_hi, n_lo, N), jnp.kron(c_hi, eye).astype(BF16), jnp.kron(s_hi, eye).astype(BF16),
      tw(np.cos(ang)), tw(np.sin(ang)))

    a_in = pl.BlockSpec((1, d, n_lo, Ec), lambda b, j, e: (b, j, 0, e))
    wide = pl.BlockSpec((n_lo, 2 * n_lo), lambda b, j, e: (0, 0))
    y = pl.pallas_call(
        functools.partial(_fnet_b_kernel, norm=1.0 / math.sqrt(S * G)),
        grid=(B, n_hi // d, E // Ec),
        in_specs=[a_in, a_in, wide, wide, mat(G), mat(G), mat(n_lo * d),
                  pl.BlockSpec((Ec // G, G, G), lambda b, j, e: (e, 0, 0))],
        out_specs=pl.BlockSpec((1, n_lo, d, Ec), lambda b, j, e: (b, 0, j, e)),
        out_shape=jax.ShapeDtypeStruct((B, n_lo, n_hi, E), F32),
        compiler_params=_params(("parallel", "parallel", "parallel"), 56),
        name="fnet_stage_b",
    )(ar, ai, jnp.concatenate([c_lo, s_lo], axis=1).astype(BF16), jnp.concatenate([-s_lo, c_lo], axis=1).astype(BF16),
      c_ch.astype(BF16), s_ch.astype(BF16),
      jnp.asarray(perm.reshape(n_lo * d, d * n_lo), BF16), group_w.astype(BF16))
    return y.reshape(B, S, E)


def conv_layer(x, norm, w_in, dw, dw_b, ln_g, ln_b, w_out):
    B, S, D = x.shape
    E = w_out.shape[0]
    u = norm_matmul(x.reshape(B * S, D), norm, w_in)
    return conv_mixer(u.reshape(B, S, 3 * E), x.reshape(B * S, D), dw, dw_b, ln_g, ln_b, w_out).reshape(B, S, D)


def hgrn_layer(x, norm, w_in, lb_fwd, lb_bwd, o_norm, w_out):
    B, S, D = x.shape
    E = w_out.shape[0]
    u = norm_matmul(x.reshape(B * S, D), norm, w_in)
    u3 = u.reshape(B, S, 5 * E)
    o_fwd = hgrn_direction(u3, lb_fwd, 1, reverse=False)
    o = hgrn_direction(u3, lb_bwd, 2, reverse=True, prev=o_fwd, o_norm=o_norm)
    return gate_out(o.reshape(B * S, E), u, 4, x.reshape(B * S, D), w_out).reshape(B, S, D)


def rope_tables(positions):
    half = ROPE_DIM // 2
    inv = 1.0 / (ROPE_THETA ** (jnp.arange(0, ROPE_DIM, 2, dtype=F32) / ROPE_DIM))
    ang = positions.astype(F32).reshape(-1, 1) * inv
    cos, sin = jnp.cos(ang), jnp.sin(ang)
    T = ang.shape[0]
    rest = LANES - ROPE_DIM
    keep = jnp.concatenate([cos, cos, jnp.ones((T, rest), F32)], axis=-1)
    partner = jnp.concatenate([-sin, sin, jnp.zeros((T, rest), F32)], axis=-1)
    return (cos.T, sin.T), (keep, partner)


def diff_layer(x, rope, norm, w_in, q_norm, k_norm, lam_q1, lam_k1, lam_q2, lam_k2, sub_norm, w_out, lam_init):
    B, S, D = x.shape
    E = w_out.shape[0]
    wq, wk, wv, wz = (w_in[:, i * E:(i + 1) * E] for i in range(4))
    qt, h = attn_projection(x, wq, "q", rope[0], q_norm, norm)
    kp = attn_projection(h, wk, "k", rope[1], k_norm)
    vt = attn_projection(h, wv, "v")
    z = matmul(h.reshape(B * S, D), wz)
    lam = jnp.exp(jnp.sum(lam_q1 * lam_k1)) - jnp.exp(jnp.sum(lam_q2 * lam_k2)) + lam_init
    o = diff_attention_core(qt, kp, vt, lam, sub_norm, 1.0 - lam_init)
    return gate_out(o.reshape(B * S, E), z, 0, x.reshape(B * S, D), w_out).reshape(B, S, D)


def fnet_layer(x, norm, w_in, group_w, w_out):
    B, S, D = x.shape
    E = w_out.shape[0]
    u = norm_matmul(x.reshape(B * S, D), norm, w_in)
    y = fourier_core(u.reshape(B, S, 2 * E), group_w)
    return gate_out(y.reshape(B * S, E), u, 1, x.reshape(B * S, D), w_out).reshape(B, S, D)


def hgrn_lower_bounds(table):
    lb = jnp.cumsum(jax.nn.softmax(table.astype(F32), axis=0), axis=0)
    return lb - lb[0:1]


def kernel(x, positions, conv_norm, conv_w_in, conv_dw, conv_dw_b, conv_ln_g, conv_ln_b, conv_w_out, hgrn_norm, hgrn_w_in, hgrn_lb_fwd, hgrn_lb_bwd, hgrn_o_norm, hgrn_w_out, diff_norm, diff_w_in, diff_q_norm, diff_k_norm, diff_lam_q1, diff_lam_k1, diff_lam_q2, diff_lam_k2, diff_sub_norm, diff_w_out, fnet_norm, fnet_w_in, fnet_group_w, fnet_w_out):
    depth = hgrn_lb_fwd.shape[0]
    n_mixers = 4
    rope = rope_tables(positions)
    lb_fwd = hgrn_lower_bounds(hgrn_lb_fwd)
    lb_bwd = hgrn_lower_bounds(hgrn_lb_bwd)
    for layer in range(depth):
        m, j = layer % n_mixers, layer // n_mixers
        if m == 0:
            x = conv_layer(x, conv_norm[j], conv_w_in[j], conv_dw[j], conv_dw_b[j], conv_ln_g[j], conv_ln_b[j],
                           conv_w_out[j])
        elif m == 1:
            x = hgrn_layer(x, hgrn_norm[j], hgrn_w_in[j], lb_fwd[layer], lb_bwd[layer], hgrn_o_norm[j],
                           hgrn_w_out[j])
        elif m == 2:
            lam_init = 0.8 - 0.6 * math.exp(-0.3 * layer)
            x = diff_layer(x, rope, diff_norm[j], diff_w_in[j], diff_q_norm[j], diff_k_norm[j], diff_lam_q1[j],
                           diff_lam_k1[j], diff_lam_q2[j], diff_lam_k2[j], diff_sub_norm[j], diff_w_out[j], lam_init)
        else:
            x = fnet_layer(x, fnet_norm[j], fnet_w_in[j], fnet_group_w[j], fnet_w_out[j])
    return x
```

```python
import functools
import math

import numpy as np
import jax
import jax.numpy as jnp
from jax import lax
from jax.experimental import pallas as pl
from jax.experimental.pallas import tpu as pltpu

EPS = 1e-6
LANES = 128
SUBLANES = 8
HEAD_DIM = 128
CONV_HALO = 16
ROPE_DIM = 32
ROPE_THETA = 500000.0
FN_GROUPS = 8
MIB = 1 << 20

F32 = jnp.float32
BF16 = jnp.bfloat16
NT_DIMS = (((1,), (1,)), ((), ()))


def _params(semantics, vmem_mib):
    return pltpu.CompilerParams(dimension_semantics=semantics, vmem_limit_bytes=vmem_mib * MIB)


def _silu(z):
    return z * jax.nn.sigmoid(z)


def _norm_matmul_kernel(x_ref, g_ref, w_ref, o_ref, h_ref):
    @pl.when(pl.program_id(1) == 0)
    def _():
        x = x_ref[...]
        ms = jnp.mean(x * x, axis=-1, keepdims=True)
        h_ref[...] = (x * lax.rsqrt(ms + EPS) * g_ref[...]).astype(h_ref.dtype)

    o_ref[...] = jnp.dot(h_ref[...], w_ref[...], preferred_element_type=F32).astype(o_ref.dtype)


def norm_matmul(x2d, gain, w, *, tm=1024, tn=2048):
    T, D = x2d.shape
    N = w.shape[1]
    tm, tn = min(tm, T), min(tn, N)
    return pl.pallas_call(
        _norm_matmul_kernel,
        grid=(T // tm, N // tn),
        in_specs=[
            pl.BlockSpec((tm, D), lambda i, j: (i, 0)),
            pl.BlockSpec((1, D), lambda i, j: (0, 0)),
            pl.BlockSpec((D, tn), lambda i, j: (0, j)),
        ],
        out_specs=pl.BlockSpec((tm, tn), lambda i, j: (i, j)),
        out_shape=jax.ShapeDtypeStruct((T, N), F32),
        scratch_shapes=[pltpu.VMEM((tm, D), BF16)],
        compiler_params=_params(("parallel", "arbitrary"), 58),
        name="norm_matmul",
    )(x2d, gain.reshape(1, D), w.astype(BF16))


def _gate_out_kernel(o_ref, z_ref, x_ref, w_ref, out_ref):
    g = (o_ref[...] * _silu(z_ref[...])).astype(BF16)
    out_ref[...] = x_ref[...] + jnp.dot(g, w_ref[...], preferred_element_type=F32)


def gate_out(o2d, u2d, z_block, x2d, w_out, *, tm=512):
    T, E = o2d.shape
    D = w_out.shape[1]
    tm = min(tm, T)
    return pl.pallas_call(
        _gate_out_kernel,
        grid=(T // tm,),
        in_specs=[
            pl.BlockSpec((tm, E), lambda i: (i, 0)),
            pl.BlockSpec((tm, E), lambda i: (i, z_block)),
            pl.BlockSpec((tm, D), lambda i: (i, 0)),
            pl.BlockSpec((E, D), lambda i: (0, 0), pipeline_mode=pl.Buffered(1)),
        ],
        out_specs=pl.BlockSpec((tm, D), lambda i: (i, 0)),
        out_shape=jax.ShapeDtypeStruct((T, D), F32),
        compiler_params=_params(("parallel",), 48),
        name="gate_out",
    )(o2d, u2d, x2d, w_out.astype(BF16))


def _conv_kernel(am_ref, ap_ref, an_ref, bm_ref, bp_ref, bn_ref, z_ref, x_ref, dw_ref, dwb_ref, g_ref, beta_ref,
                 w_ref, o_ref, vbuf, shifted, cbuf, *, taps, row_block, col_block):
    s = pl.program_id(1)
    ts, E = am_ref.shape[1], am_ref.shape[2]
    glu = lambda a, b: a * jax.nn.sigmoid(b)
    vbuf[0:CONV_HALO, :] = jnp.where(s > 0, glu(ap_ref[0], bp_ref[0]), 0.0)
    vbuf[CONV_HALO:CONV_HALO + ts, :] = glu(am_ref[0], bm_ref[0])
    vbuf[CONV_HALO + ts:, :] = jnp.where(s < pl.num_programs(1) - 1, glu(an_ref[0], bn_ref[0]), 0.0)
    span = shifted.shape[1]
    for r in range(1, SUBLANES):
        shifted[r - 1] = vbuf[r:r + span, :]
    first = CONV_HALO - taps // 2
    for r0 in range(0, ts, row_block):
        for c0 in range(0, E, col_block):
            cols = slice(c0, c0 + col_block)
            acc = jnp.zeros((row_block, col_block), F32)
            for k in range(taps):
                phase, base = (first + k) % SUBLANES, r0 + (first + k) // SUBLANES * SUBLANES
                win = (vbuf[base:base + row_block, cols] if phase == 0
                       else shifted[phase - 1, base:base + row_block, cols])
                acc = acc + dw_ref[k:k + 1, cols] * win
            cbuf[r0:r0 + row_block, cols] = acc + dwb_ref[:, cols]
    c = cbuf[...]
    mu = jnp.mean(c, axis=-1, keepdims=True)
    cc = c - mu
    var = jnp.mean(cc * cc, axis=-1, keepdims=True)
    y = cc * lax.rsqrt(var + EPS) * g_ref[...] + beta_ref[...]
    gated = (_silu(y) * _silu(z_ref[0])).astype(BF16)
    o_ref[...] = x_ref[...] + jnp.dot(gated, w_ref[...], preferred_element_type=F32)


def conv_mixer(u, x2d, dw, dw_b, ln_g, ln_b, w_out, *, ts=256):
    B, S, N = u.shape
    E = N // 3
    D = w_out.shape[1]
    taps = dw.shape[0]
    ts = min(ts, S)
    hb = ts // CONV_HALO
    st = S // ts
    last_halo = S // CONV_HALO - 1
    main = lambda col: pl.BlockSpec((1, ts, E), lambda b, s: (b, s, col))
    prev = lambda col: pl.BlockSpec((1, CONV_HALO, E), lambda b, s: (b, jnp.maximum(s * hb - 1, 0), col))
    nxt = lambda col: pl.BlockSpec((1, CONV_HALO, E), lambda b, s: (b, jnp.minimum((s + 1) * hb, last_halo), col))
    vec = lambda rows: pl.BlockSpec((rows, E), lambda b, s: (0, 0))
    rows = pl.BlockSpec((ts, D), lambda b, s: (b * st + s, 0))
    return pl.pallas_call(
        functools.partial(_conv_kernel, taps=taps, row_block=32, col_block=512),
        grid=(B, st),
        in_specs=[main(0), prev(0), nxt(0), main(1), prev(1), nxt(1), main(2), rows, vec(taps), vec(1), vec(1), vec(1),
                  pl.BlockSpec((E, D), lambda b, s: (0, 0), pipeline_mode=pl.Buffered(1))],
        out_specs=rows,
        out_shape=jax.ShapeDtypeStruct((B * S, D), F32),
        scratch_shapes=[pltpu.VMEM((ts + 2 * CONV_HALO, E), F32),
                        pltpu.VMEM((SUBLANES - 1, ts + 2 * CONV_HALO - SUBLANES, E), F32),
                        pltpu.VMEM((ts, E), F32)],
        compiler_params=_params(("parallel", "parallel"), 58),
        name="conv_mixer",
    )(u, u, u, u, u, u, u, x2d, dw, dw_b.reshape(1, E), ln_g.reshape(1, E), ln_b.reshape(1, E), w_out.astype(BF16))


def _hgrn_kernel(*refs, reverse, finalize, chunk):
    if finalize:
        q_ref, a_ref, v_ref, lb_ref, prev_ref, gn_ref, o_ref, st_ref = refs
    else:
        q_ref, a_ref, v_ref, lb_ref, o_ref, st_ref = refs
    C = chunk
    n = q_ref.shape[1] // C
    heads = q_ref.shape[2] // HEAD_DIM

    @pl.when(pl.program_id(2) == 0)
    def _():
        st_ref[...] = jnp.zeros_like(st_ref)

    row = lax.broadcasted_iota(jnp.int32, (C, HEAD_DIM), 0)
    pair_xor = lax.broadcasted_iota(jnp.int32, (C, C), 0) ^ lax.broadcasted_iota(jnp.int32, (C, C), 1)
    tt, ss = lax.broadcasted_iota(jnp.int32, (C, C), 0), lax.broadcasted_iota(jnp.int32, (C, C), 1)
    pair_level = jnp.where((tt < ss) if reverse else (tt > ss),
                           (pltpu.bitcast(pair_xor.astype(F32), jnp.int32) >> 23) - 127, -1)

    def one_chunk(j, carry):
        r0 = pl.multiple_of((n - 1 - j if reverse else j) * C, C)
        rows = pl.ds(r0, C)
        q, kk, v, near, far, tot, scores = [], [], [], [], [], [], []
        for h in range(heads):
            lanes = slice(h * HEAD_DIM, (h + 1) * HEAD_DIM)
            a, lb = a_ref[0, rows, lanes], lb_ref[:, lanes]
            e = jnp.exp(-jnp.abs(a))
            r = 1.0 / (1.0 + e)
            er = e * r
            pos = a >= 0
            f = lb + (1.0 - lb) * jnp.where(pos, r, er)
            q.append(q_ref[0, rows, lanes])
            v.append(v_ref[0, rows, lanes])
            kk.append((1.0 - lb) * jnp.where(pos, er, r))
            near.append(f)
            far.append(jnp.ones_like(f))
            tot.append(f)
            scores.append(jnp.zeros((C, C), F32))

        m, level = 1, 0
        while m < C:
            bit = (row & m) != 0
            for h in range(heads):
                qm = (q[h] * near[h]).astype(BF16)
                km = (kk[h] * far[h]).astype(BF16)
                sc = lax.dot_general(qm, km, NT_DIMS, preferred_element_type=F32)
                scores[h] = jnp.where(pair_level == level, sc, scores[h])
                tot_before = pltpu.roll(tot[h], m, 0)
                tot_after = pltpu.roll(tot[h], C - m, 0)
                if reverse:
                    near[h] = jnp.where(bit, near[h], near[h] * tot_after)
                    far[h] = jnp.where(bit, far[h] * tot_before, far[h])
                else:
                    near[h] = jnp.where(bit, near[h] * tot_before, near[h])
                    far[h] = jnp.where(bit, far[h], far[h] * tot_after)
                tot[h] = tot[h] * jnp.where(bit, tot_before, tot_after)
            m *= 2
            level += 1

        for h in range(heads):
            lanes = slice(h * HEAD_DIM, (h + 1) * HEAD_DIM)
            vb = v[h].astype(BF16)
            st = st_ref[h]
            o = jnp.dot(scores[h].astype(BF16), vb, preferred_element_type=F32)
            o = o + jnp.sum(q[h] * kk[h], axis=1, keepdims=True) * v[h]
            o = o + lax.dot_general((q[h] * near[h]).astype(BF16), st.astype(BF16), NT_DIMS,
                                    preferred_element_type=F32)
            st_ref[h] = st * tot[h][0:1] + jnp.dot(v[h].T.astype(BF16), (kk[h] * far[h]).astype(BF16),
                                                   preferred_element_type=F32)
            if finalize:
                o = o + prev_ref[0, rows, lanes]
                ms = jnp.mean(o * o, axis=-1, keepdims=True)
                o = o * lax.rsqrt(ms + EPS) * gn_ref[...]
            o_ref[0, rows, lanes] = o.astype(o_ref.dtype)
        return carry

    lax.fori_loop(0, n, one_chunk, 0)


def hgrn_direction(u, lb, gate_block, *, reverse, prev=None, o_norm=None, rows=512, chunk=128, heads=16):
    B, S, N = u.shape
    E = N // 5
    W = heads * HEAD_DIM
    HB = E // W
    R = min(rows, S)
    nb = S // R
    finalize = prev is not None
    ridx = (lambda c: nb - 1 - c) if reverse else (lambda c: c)
    col = lambda blk: pl.BlockSpec((1, R, W), lambda b, h, c: (b, ridx(c), blk * HB + h))
    in_specs = [col(0), col(gate_block), col(3), pl.BlockSpec((1, W), lambda b, h, c: (0, h))]
    args = [u, u, u, lb.reshape(1, E)]
    if finalize:
        in_specs += [pl.BlockSpec((1, R, W), lambda b, h, c: (b, ridx(c), h)),
                     pl.BlockSpec((1, HEAD_DIM), lambda b, h, c: (0, 0))]
        args += [prev, o_norm.reshape(1, HEAD_DIM)]
    return pl.pallas_call(
        functools.partial(_hgrn_kernel, reverse=reverse, finalize=finalize, chunk=min(chunk, R)),
        grid=(B, HB, nb),
        in_specs=in_specs,
        out_specs=pl.BlockSpec((1, R, W), lambda b, h, c: (b, ridx(c), h)),
        out_shape=jax.ShapeDtypeStruct((B, S, E), BF16 if finalize else F32),
        scratch_shapes=[pltpu.VMEM((heads, HEAD_DIM, HEAD_DIM), F32)],
        compiler_params=_params(("parallel", "parallel", "arbitrary"), 56),
        name="hgrn_bwd" if reverse else "hgrn_fwd",
    )(*args)


def _attn_proj_kernel(*refs, mode, scale, slab):
    if mode == "q":
        x_ref, g_ref, w_ref, t1_ref, t2_ref, gn_ref, o_ref, h_ref = refs

        @pl.when(pl.program_id(1) == 0)
        def _():
            x = x_ref[...]
            ms = jnp.mean(x * x, axis=-1, keepdims=True)
            h_ref[...] = (x * lax.rsqrt(ms + EPS) * g_ref[...]).astype(h_ref.dtype)
    elif mode == "k":
        h_ref, w_ref, t1_ref, t2_ref, gn_ref, swap_ref, o_ref = refs
    else:
        h_ref, w_ref, o_ref = refs

    half = ROPE_DIM // 2
    project = lambda c0: jnp.dot(h_ref[...], w_ref[:, c0:c0 + slab], preferred_element_type=F32)
    tn = w_ref.shape[1]
    nxt = project(0)
    for c0 in range(0, tn, slab):
        acc, nxt = nxt, (project(c0 + slab) if c0 + slab < tn else None)
        for g0 in range(0, slab, HEAD_DIM):
            y = acc[:, g0:g0 + HEAD_DIM]
            cols = slice(c0 + g0, c0 + g0 + HEAD_DIM)
            if mode == "q":
                yt = y.T
                yt = yt * lax.rsqrt(jnp.mean(yt * yt, axis=0, keepdims=True) + EPS) * (gn_ref[...] * scale)
                lo, hi, cos, sin = yt[:half], yt[half:ROPE_DIM], t1_ref[...], t2_ref[...]
                yt = jnp.concatenate([lo * cos - hi * sin, hi * cos + lo * sin, yt[ROPE_DIM:]], axis=0)
                o_ref[0, cols, :] = yt.astype(o_ref.dtype)
            elif mode == "k":
                yn = y * lax.rsqrt(jnp.mean(y * y, axis=-1, keepdims=True) + EPS) * gn_ref[...]
                partner = jnp.dot(yn.astype(BF16), swap_ref[...], preferred_element_type=F32)
                o_ref[:, cols] = (yn * t1_ref[...] + partner * t2_ref[...]).astype(o_ref.dtype)
            else:
                o_ref[0, cols, :] = y.T.astype(o_ref.dtype)


def attn_projection(x, w, mode, tables=None, head_gain=None, gain=None, *, tm=1024, tn=2048):
    B, S, D = x.shape
    E = w.shape[1]
    tm, tn = min(tm, S), min(tn, E)
    sb = S // tm
    if mode == "k":
        out_specs = [pl.BlockSpec((tm, tn), lambda i, j: (i, j))]
        out_shape = [jax.ShapeDtypeStruct((B * S, E), BF16)]
    else:
        out_specs = [pl.BlockSpec((1, tn, tm), lambda i, j: (i // sb, j, i % sb))]
        out_shape = [jax.ShapeDtypeStruct((B, E, S), BF16)]
    rows = pl.BlockSpec((tm, D), lambda i, j: (i, 0))
    weights = pl.BlockSpec((D, tn), lambda i, j: (0, j))
    half = ROPE_DIM // 2
    if mode == "q":
        in_specs = [rows, pl.BlockSpec((1, D), lambda i, j: (0, 0)), weights,
                    pl.BlockSpec((half, tm), lambda i, j: (0, i)), pl.BlockSpec((half, tm), lambda i, j: (0, i)),
                    pl.BlockSpec((HEAD_DIM, 1), lambda i, j: (0, 0))]
        args = [x.reshape(B * S, D), gain.reshape(1, D), w.astype(BF16), *tables, head_gain.reshape(HEAD_DIM, 1)]
        out_specs.append(rows)
        out_shape.append(jax.ShapeDtypeStruct((B * S, D), BF16))
    elif mode == "k":
        swap = np.zeros((HEAD_DIM, HEAD_DIM), np.float32)
        swap[np.arange(half) + half, np.arange(half)] = 1.0
        swap[np.arange(half), np.arange(half) + half] = 1.0
        tab = pl.BlockSpec((tm, LANES), lambda i, j: (i, 0))
        in_specs = [rows, weights, tab, tab, pl.BlockSpec((1, HEAD_DIM), lambda i, j: (0, 0)),
                    pl.BlockSpec((HEAD_DIM, HEAD_DIM), lambda i, j: (0, 0))]
        args = [x.reshape(B * S, D), w.astype(BF16), *tables, head_gain.reshape(1, HEAD_DIM), jnp.asarray(swap, BF16)]
    else:
        in_specs = [rows, weights]
        args = [x.reshape(B * S, D), w.astype(BF16)]
    out = pl.pallas_call(
        functools.partial(_attn_proj_kernel, mode=mode, scale=math.log2(math.e) / math.sqrt(HEAD_DIM),
                          slab=min(2 * HEAD_DIM, tn)),
        grid=(B * S // tm, E // tn),
        in_specs=in_specs,
        out_specs=out_specs,
        out_shape=out_shape,
        compiler_params=_params(("parallel", "arbitrary"), 58),
        name="attn_proj_" + mode,
    )(*args)
    if mode == "q":
        return out[0], out[1].reshape(B, S, D)
    return out[0].reshape(B, S, E) if mode == "k" else out[0]


def _matmul_kernel(h_ref, w_ref, o_ref):
    o_ref[...] = jnp.dot(h_ref[...], w_ref[...], preferred_element_type=F32)


def matmul(h2d, w, *, tm=1024, tn=2048):
    T, D = h2d.shape
    N = w.shape[1]
    tm, tn = min(tm, T), min(tn, N)
    return pl.pallas_call(
        _matmul_kernel,
        grid=(T // tm, N // tn),
        in_specs=[pl.BlockSpec((tm, D), lambda i, j: (i, 0)), pl.BlockSpec((D, tn), lambda i, j: (0, j))],
        out_specs=pl.BlockSpec((tm, tn), lambda i, j: (i, j)),
        out_shape=jax.ShapeDtypeStruct((T, N), F32),
        compiler_params=_params(("parallel", "parallel"), 48),
        name="matmul",
    )(h2d, w.astype(BF16))


def _attn_kernel(lam_ref, qt_ref, k_ref, vt_ref, sn_ref, o_ref, m_ref, l_ref, acc_ref, s00, s01, s10, s11, mc_ref,
                 *, post_scale, tq, tkc):
    tile = pl.program_id(2)
    n = k_ref.shape[1] // tkc
    s_refs = ((s00, s01), (s10, s11))
    m_ref[...] = jnp.full_like(m_ref, -jnp.inf)
    l_ref[...] = jnp.zeros_like(l_ref)
    acc_ref[...] = jnp.zeros_like(acc_ref)

    def produce(q_tile, c, slot, i):
        sl = slice(i * HEAD_DIM, (i + 1) * HEAD_DIM)
        off = pl.multiple_of(c * tkc, tkc)
        q = qt_ref[0, sl, pl.ds(pl.multiple_of(q_tile * tq, tq), tq)]
        s_new = jnp.dot(k_ref[0, pl.ds(off, tkc), sl], q, preferred_element_type=F32)
        s_refs[slot][i][...] = s_new
        mc_ref[slot, i] = jnp.max(s_new, axis=0, keepdims=True)

    def consume(vt_c, slot, i):
        m_prev = m_ref[i]
        m_new = jnp.maximum(m_prev, mc_ref[slot, i])
        alpha = jnp.exp2(m_prev - m_new)
        p = jnp.exp2(s_refs[slot][i][...] - m_new)
        l_ref[i] = alpha * l_ref[i] + jnp.sum(p, axis=0, keepdims=True)
        acc_ref[i] = alpha * acc_ref[i] + jnp.dot(vt_c, p.astype(BF16), preferred_element_type=F32)
        m_ref[i] = m_new

    def step(c, slot, next_tile, next_chunk):
        vt_c = vt_ref[0, :, pl.ds(pl.multiple_of(c * tkc, tkc), tkc)]
        for i in range(2):
            produce(next_tile, next_chunk, 1 - slot, i)
            consume(vt_c, slot, i)

    @pl.when(tile == 0)
    def _():
        produce(0, 0, 0, 0)
        produce(0, 0, 0, 1)

    def pair(j, carry):
        step(2 * j, 0, tile, 2 * j + 1)
        step(2 * j + 1, 1, tile, 2 * j + 2)
        return carry

    lax.fori_loop(0, n // 2 - 1, pair, 0)
    step(n - 2, 0, tile, n - 1)
    step(n - 1, 1, jnp.minimum(tile + 1, pl.num_programs(2) - 1), 0)
    o = (acc_ref[0] / l_ref[0] - lam_ref[0] * (acc_ref[1] / l_ref[1])).T
    ms = jnp.mean(o * o, axis=-1, keepdims=True)
    o_ref[0] = (o * lax.rsqrt(ms + EPS) * sn_ref[...] * post_scale).astype(o_ref.dtype)


def diff_attention_core(qt, kp, vt, lam, sub_norm, post_scale, *, tq=512, tkc=1024):
    B, S, E = kp.shape
    W = 2 * HEAD_DIM
    H = E // W
    tq, tkc = min(tq, S), min(tkc, S // 2)
    assert S % (2 * tkc) == 0
    return pl.pallas_call(
        functools.partial(_attn_kernel, post_scale=post_scale, tq=tq, tkc=tkc),
        grid=(B, H, S // tq),
        in_specs=[
            pl.BlockSpec(memory_space=pltpu.SMEM),
            pl.BlockSpec((1, W, S), lambda b, h, i: (b, h, 0), pipeline_mode=pl.Buffered(1)),
            pl.BlockSpec((1, S, W), lambda b, h, i: (b, 0, h)),
            pl.BlockSpec((1, W, S), lambda b, h, i: (b, h, 0)),
            pl.BlockSpec((1, W), lambda b, h, i: (0, 0)),
        ],
        out_specs=pl.BlockSpec((1, tq, W), lambda b, h, i: (b, i, h)),
        out_shape=jax.ShapeDtypeStruct((B, S, E), BF16),
        scratch_shapes=[pltpu.VMEM((2, 1, tq), F32), pltpu.VMEM((2, 1, tq), F32), pltpu.VMEM((2, W, tq), F32)]
        + [pltpu.VMEM((tkc, tq), F32)] * 4 + [pltpu.VMEM((2, 2, 1, tq), F32)],
        compiler_params=_params(("parallel", "parallel", "arbitrary"), 56),
        name="diff_attention",
    )(lam.reshape(1), qt, kp, vt, sub_norm.reshape(1, W))


def _dft_tables(n):
    ang = 2.0 * np.pi * ((np.arange(n)[:, None] * np.arange(n)[None, :]) % n) / n
    return jnp.asarray(np.cos(ang), F32), jnp.asarray(np.sin(ang), F32)


def _fnet_a_kernel(x_ref, kc_ref, ks_ref, tc_ref, ts_ref, ar_ref, ai_ref):
    n_hi, digits, ec = x_ref.shape[1], x_ref.shape[2], x_ref.shape[3]
    x = x_ref[0].reshape(n_hi * digits, ec).astype(BF16)
    ar = jnp.dot(kc_ref[...], x, preferred_element_type=F32)
    ai = -jnp.dot(ks_ref[...], x, preferred_element_type=F32)
    tc, ts = tc_ref[0], ts_ref[0]
    ar_ref[0] = (ar * tc + ai * ts).reshape(n_hi, digits, ec)
    ai_ref[0] = (ai * tc - ar * ts).reshape(n_hi, digits, ec)


def _fnet_b_kernel(ar_ref, ai_ref, wr_ref, wi_ref, cc_ref, sc_ref, perm_ref, gw_ref, y_ref, *, norm):
    wr, wi = wr_ref[...], wi_ref[...]
    digits, n_lo = ar_ref.shape[1], ar_ref.shape[2]
    G = gw_ref.shape[2]
    br, bi = [], []
    for j in range(digits):
        a = jnp.concatenate([ar_ref[0, j], ai_ref[0, j]], axis=0).astype(BF16)
        br.append(jnp.dot(wr, a, preferred_element_type=F32).astype(BF16))
        bi.append(jnp.dot(wi, a, preferred_element_type=F32).astype(BF16))
    br, bi = jnp.concatenate(br, axis=0), jnp.concatenate(bi, axis=0)
    for g in range(gw_ref.shape[0]):
        sl = slice(g * G, (g + 1) * G)
        f = (jnp.dot(br[:, sl], cc_ref[...], preferred_element_type=F32)
             + jnp.dot(bi[:, sl], sc_ref[...], preferred_element_type=F32)) * norm
        f = jnp.dot(perm_ref[...], f.astype(BF16), preferred_element_type=F32).astype(BF16)
        y_ref[0, :, :, sl] = jnp.dot(f, gw_ref[g], preferred_element_type=F32).reshape(n_lo, digits, G)


def fourier_core(u, group_w, *, n_lo=128, e_chunk=1024):
    B, S, N = u.shape
    E = N // 2
    G = E // FN_GROUPS
    n_hi = S // n_lo
    Ec = min(e_chunk, E)
    d = SUBLANES
    assert n_hi % d == 0 and n_lo % d == 0
    c_hi, s_hi = _dft_tables(n_hi)
    c_lo, s_lo = _dft_tables(n_lo)
    c_ch, s_ch = _dft_tables(G)
    eye = jnp.eye(d, dtype=F32)
    ang = 2.0 * np.pi * ((np.arange(n_lo)[:, None] * np.arange(n_hi)[None, :]) % S) / S
    tw = lambda t: jnp.asarray(t, F32).reshape(n_lo // d, d, n_hi).transpose(0, 2, 1).reshape(n_lo // d, n_hi * d, 1)
    perm = np.zeros((n_lo, d, d, n_lo), np.float32)
    perm[np.arange(n_lo)[:, None], np.arange(d)[None, :], np.arange(d)[None, :], np.arange(n_lo)[:, None]] = 1.0
    mat = lambda n: pl.BlockSpec((n, n), lambda b, j, e: (0, 0))
    tws = pl.BlockSpec((1, n_hi * d, 1), lambda b, j, e: (j, 0, 0))
    a_blk = pl.BlockSpec((1, n_hi, d, Ec), lambda b, j, e: (b, 0, j, e))
    ar, ai = pl.pallas_call(
        _fnet_a_kernel,
        grid=(B, n_lo // d, E // Ec),
        in_specs=[a_blk, mat(n_hi * d), mat(n_hi * d), tws, tws],
        out_specs=[a_blk, a_blk],
        out_shape=[jax.ShapeDtypeStruct((B, n_hi, n_lo, E), F32)] * 2,
        compiler_params=_params(("parallel", "parallel", "parallel"), 56),
        name="fnet_stage_a",
    )(u.reshape(B, n_hi, n_lo, N), jnp.kron(c_hi, eye).astype(BF16), jnp.kron(s_hi, eye).astype(BF16),
      tw(np.cos(ang)), tw(np.sin(ang)))

    a_in = pl.BlockSpec((1, d, n_lo, Ec), lambda b, j, e: (b, j, 0, e))
    wide = pl.BlockSpec((n_lo, 2 * n_lo), lambda b, j, e: (0, 0))
    y = pl.pallas_call(
        functools.partial(_fnet_b_kernel, norm=1.0 / math.sqrt(S * G)),
        grid=(B, n_hi // d, E // Ec),
        in_specs=[a_in, a_in, wide, wide, mat(G), mat(G), mat(n_lo * d),
                  pl.BlockSpec((Ec // G, G, G), lambda b, j, e: (e, 0, 0))],
        out_specs=pl.BlockSpec((1, n_lo, d, Ec), lambda b, j, e: (b, 0, j, e)),
        out_shape=jax.ShapeDtypeStruct((B, n_lo, n_hi, E), F32),
        compiler_params=_params(("parallel", "parallel", "parallel"), 56),
        name="fnet_stage_b",
    )(ar, ai, jnp.concatenate([c_lo, s_lo], axis=1).astype(BF16), jnp.concatenate([-s_lo, c_lo], axis=1).astype(BF16),
      c_ch.astype(BF16), s_ch.astype(BF16),
      jnp.asarray(perm.reshape(n_lo * d, d * n_lo), BF16), group_w.astype(BF16))
    return y.reshape(B, S, E)


def conv_layer(x, norm, w_in, dw, dw_b, ln_g, ln_b, w_out):
    B, S, D = x.shape
    E = w_out.shape[0]
    u = norm_matmul(x.reshape(B * S, D), norm, w_in)
    return conv_mixer(u.reshape(B, S, 3 * E), x.reshape(B * S, D), dw, dw_b, ln_g, ln_b, w_out).reshape(B, S, D)


def hgrn_layer(x, norm, w_in, lb_fwd, lb_bwd, o_norm, w_out):
    B, S, D = x.shape
    E = w_out.shape[0]
    u = norm_matmul(x.reshape(B * S, D), norm, w_in)
    u3 = u.reshape(B, S, 5 * E)
    o_fwd = hgrn_direction(u3, lb_fwd, 1, reverse=False)
    o = hgrn_direction(u3, lb_bwd, 2, reverse=True, prev=o_fwd, o_norm=o_norm)
    return gate_out(o.reshape(B * S, E), u, 4, x.reshape(B * S, D), w_out).reshape(B, S, D)


def rope_tables(positions):
    half = ROPE_DIM // 2
    inv = 1.0 / (ROPE_THETA ** (jnp.arange(0, ROPE_DIM, 2, dtype=F32) / ROPE_DIM))
    ang = positions.astype(F32).reshape(-1, 1) * inv
    cos, sin = jnp.cos(ang), jnp.sin(ang)
    T = ang.shape[0]
    rest = LANES - ROPE_DIM
    keep = jnp.concatenate([cos, cos, jnp.ones((T, rest), F32)], axis=-1)
    partner = jnp.concatenate([-sin, sin, jnp.zeros((T, rest), F32)], axis=-1)
    return (cos.T, sin.T), (keep, partner)


def diff_layer(x, rope, norm, w_in, q_norm, k_norm, lam_q1, lam_k1, lam_q2, lam_k2, sub_norm, w_out, lam_init):
    B, S, D = x.shape
    E = w_out.shape[0]
    wq, wk, wv, wz = (w_in[:, i * E:(i + 1) * E] for i in range(4))
    qt, h = attn_projection(x, wq, "q", rope[0], q_norm, norm)
    kp = attn_projection(h, wk, "k", rope[1], k_norm)
    vt = attn_projection(h, wv, "v")
    z = matmul(h.reshape(B * S, D), wz)
    lam = jnp.exp(jnp.sum(lam_q1 * lam_k1)) - jnp.exp(jnp.sum(lam_q2 * lam_k2)) + lam_init
    o = diff_attention_core(qt, kp, vt, lam, sub_norm, 1.0 - lam_init)
    return gate_out(o.reshape(B * S, E), z, 0, x.reshape(B * S, D), w_out).reshape(B, S, D)


def fnet_layer(x, norm, w_in, group_w, w_out):
    B, S, D = x.shape
    E = w_out.shape[0]
    u = norm_matmul(x.reshape(B * S, D), norm, w_in)
    y = fourier_core(u.reshape(B, S, 2 * E), group_w)
    return gate_out(y.reshape(B * S, E), u, 1, x.reshape(B * S, D), w_out).reshape(B, S, D)


def hgrn_lower_bounds(table):
    lb = jnp.cumsum(jax.nn.softmax(table.astype(F32), axis=0), axis=0)
    return lb - lb[0:1]


def kernel(x, positions, conv_norm, conv_w_in, conv_dw, conv_dw_b, conv_ln_g, conv_ln_b, conv_w_out, hgrn_norm, hgrn_w_in, hgrn_lb_fwd, hgrn_lb_bwd, hgrn_o_norm, hgrn_w_out, diff_norm, diff_w_in, diff_q_norm, diff_k_norm, diff_lam_q1, diff_lam_k1, diff_lam_q2, diff_lam_k2, diff_sub_norm, diff_w_out, fnet_norm, fnet_w_in, fnet_group_w, fnet_w_out):
    depth = hgrn_lb_fwd.shape[0]
    n_mixers = 4
    rope = rope_tables(positions)
    lb_fwd = hgrn_lower_bounds(hgrn_lb_fwd)
    lb_bwd = hgrn_lower_bounds(hgrn_lb_bwd)
    for layer in range(depth):
        m, j = layer % n_mixers, layer // n_mixers
        if m == 0:
            x = conv_layer(x, conv_norm[j], conv_w_in[j], conv_dw[j], conv_dw_b[j], conv_ln_g[j], conv_ln_b[j],
                           conv_w_out[j])
        elif m == 1:
            x = hgrn_layer(x, hgrn_norm[j], hgrn_w_in[j], lb_fwd[layer], lb_bwd[layer], hgrn_o_norm[j],
                           hgrn_w_out[j])
        elif m == 2:
            lam_init = 0.8 - 0.6 * math.exp(-0.3 * layer)
            x = diff_layer(x, rope, diff_norm[j], diff_w_in[j], diff_q_norm[j], diff_k_norm[j], diff_lam_q1[j],
                           diff_lam_k1[j], diff_lam_q2[j], diff_lam_k2[j], diff_sub_norm[j], diff_w_out[j], lam_init)
        else:
            x = fnet_layer(x, fnet_norm[j], fnet_w_in[j], fnet_group_w[j], fnet_w_out[j])
    return x
```

```python
import functools
import math

import numpy as np
import jax
import jax.numpy as jnp
from jax import lax
from jax.experimental import pallas as pl
from jax.experimental.pallas import tpu as pltpu

EPS = 1e-6
LANES = 128
SUBLANES = 8
HEAD_DIM = 128
CONV_HALO = 16
ROPE_DIM = 32
ROPE_THETA = 500000.0
FN_GROUPS = 8
MIB = 1 << 20

F32 = jnp.float32
BF16 = jnp.bfloat16
NT_DIMS = (((1,), (1,)), ((), ()))


def _params(semantics, vmem_mib):
    return pltpu.CompilerParams(dimension_semantics=semantics, vmem_limit_bytes=vmem_mib * MIB)


def _silu(z):
    return z * jax.nn.sigmoid(z)


def _norm_matmul_kernel(x_ref, g_ref, w_ref, o_ref, h_ref):
    @pl.when(pl.program_id(1) == 0)
    def _():
        x = x_ref[...]
        ms = jnp.mean(x * x, axis=-1, keepdims=True)
        h_ref[...] = (x * lax.rsqrt(ms + EPS) * g_ref[...]).astype(h_ref.dtype)

    o_ref[...] = jnp.dot(h_ref[...], w_ref[...], preferred_element_type=F32).astype(o_ref.dtype)


def norm_matmul(x2d, gain, w, *, tm=1024, tn=2048):
    T, D = x2d.shape
    N = w.shape[1]
    tm, tn = min(tm, T), min(tn, N)
    return pl.pallas_call(
        _norm_matmul_kernel,
        grid=(T // tm, N // tn),
        in_specs=[
            pl.BlockSpec((tm, D), lambda i, j: (i, 0)),
            pl.BlockSpec((1, D), lambda i, j: (0, 0)),
            pl.BlockSpec((D, tn), lambda i, j: (0, j)),
        ],
        out_specs=pl.BlockSpec((tm, tn), lambda i, j: (i, j)),
        out_shape=jax.ShapeDtypeStruct((T, N), F32),
        scratch_shapes=[pltpu.VMEM((tm, D), BF16)],
        compiler_params=_params(("parallel", "arbitrary"), 58),
        name="norm_matmul",
    )(x2d, gain.reshape(1, D), w.astype(BF16))


def _gate_out_kernel(o_ref, z_ref, x_ref, w_ref, out_ref):
    g = (o_ref[...] * _silu(z_ref[...])).astype(BF16)
    out_ref[...] = x_ref[...] + jnp.dot(g, w_ref[...], preferred_element_type=F32)


def gate_out(o2d, u2d, z_block, x2d, w_out, *, tm=512):
    T, E = o2d.shape
    D = w_out.shape[1]
    tm = min(tm, T)
    return pl.pallas_call(
        _gate_out_kernel,
        grid=(T // tm,),
        in_specs=[
            pl.BlockSpec((tm, E), lambda i: (i, 0)),
            pl.BlockSpec((tm, E), lambda i: (i, z_block)),
            pl.BlockSpec((tm, D), lambda i: (i, 0)),
            pl.BlockSpec((E, D), lambda i: (0, 0), pipeline_mode=pl.Buffered(1)),
        ],
        out_specs=pl.BlockSpec((tm, D), lambda i: (i, 0)),
        out_shape=jax.ShapeDtypeStruct((T, D), F32),
        compiler_params=_params(("parallel",), 48),
        name="gate_out",
    )(o2d, u2d, x2d, w_out.astype(BF16))


def _conv_kernel(am_ref, ap_ref, an_ref, bm_ref, bp_ref, bn_ref, z_ref, x_ref, dw_ref, dwb_ref, g_ref, beta_ref,
                 w_ref, o_ref, vbuf, shifted, cbuf, *, taps, row_block, col_block):
    s = pl.program_id(1)
    ts, E = am_ref.shape[1], am_ref.shape[2]
    glu = lambda a, b: a * jax.nn.sigmoid(b)
    vbuf[0:CONV_HALO, :] = jnp.where(s > 0, glu(ap_ref[0], bp_ref[0]), 0.0)
    vbuf[CONV_HALO:CONV_HALO + ts, :] = glu(am_ref[0], bm_ref[0])
    vbuf[CONV_HALO + ts:, :] = jnp.where(s < pl.num_programs(1) - 1, glu(an_ref[0], bn_ref[0]), 0.0)
    span = shifted.shape[1]
    for r in range(1, SUBLANES):
        shifted[r - 1] = vbuf[r:r + span, :]
    first = CONV_HALO - taps // 2
    for r0 in range(0, ts, row_block):
        for c0 in range(0, E, col_block):
            cols = slice(c0, c0 + col_block)
            acc = jnp.zeros((row_block, col_block), F32)
            for k in range(taps):
                phase, base = (first + k) % SUBLANES, r0 + (first + k) // SUBLANES * SUBLANES
                win = (vbuf[base:base + row_block, cols] if phase == 0
                       else shifted[phase - 1, base:base + row_block, cols])
                acc = acc + dw_ref[k:k + 1, cols] * win
            cbuf[r0:r0 + row_block, cols] = acc + dwb_ref[:, cols]
    c = cbuf[...]
    mu = jnp.mean(c, axis=-1, keepdims=True)
    cc = c - mu
    var = jnp.mean(cc * cc, axis=-1, keepdims=True)
    y = cc * lax.rsqrt(var + EPS) * g_ref[...] + beta_ref[...]
    gated = (_silu(y) * _silu(z_ref[0])).astype(BF16)
    o_ref[...] = x_ref[...] + jnp.dot(gated, w_ref[...], preferred_element_type=F32)


def conv_mixer(u, x2d, dw, dw_b, ln_g, ln_b, w_out, *, ts=256):
    B, S, N = u.shape
    E = N // 3
    D = w_out.shape[1]
    taps = dw.shape[0]
    ts = min(ts, S)
    hb = ts // CONV_HALO
    st = S // ts
    last_halo = S // CONV_HALO - 1
    main = lambda col: pl.BlockSpec((1, ts, E), lambda b, s: (b, s, col))
    prev = lambda col: pl.BlockSpec((1, CONV_HALO, E), lambda b, s: (b, jnp.maximum(s * hb - 1, 0), col))
    nxt = lambda col: pl.BlockSpec((1, CONV_HALO, E), lambda b, s: (b, jnp.minimum((s + 1) * hb, last_halo), col))
    vec = lambda rows: pl.BlockSpec((rows, E), lambda b, s: (0, 0))
    rows = pl.BlockSpec((ts, D), lambda b, s: (b * st + s, 0))
    return pl.pallas_call(
        functools.partial(_conv_kernel, taps=taps, row_block=128, col_block=128),
        grid=(B, st),
        in_specs=[main(0), prev(0), nxt(0), main(1), prev(1), nxt(1), main(2), rows, vec(taps), vec(1), vec(1), vec(1),
                  pl.BlockSpec((E, D), lambda b, s: (0, 0), pipeline_mode=pl.Buffered(1))],
        out_specs=rows,
        out_shape=jax.ShapeDtypeStruct((B * S, D), F32),
        scratch_shapes=[pltpu.VMEM((ts + 2 * CONV_HALO, E), F32),
                        pltpu.VMEM((SUBLANES - 1, ts + 2 * CONV_HALO - SUBLANES, E), F32),
                        pltpu.VMEM((ts, E), F32)],
        compiler_params=_params(("parallel", "parallel"), 58),
        name="conv_mixer",
    )(u, u, u, u, u, u, u, x2d, dw, dw_b.reshape(1, E), ln_g.reshape(1, E), ln_b.reshape(1, E), w_out.astype(BF16))


def _hgrn_kernel(*refs, reverse, finalize, chunk):
    if finalize:
        q_ref, a_ref, v_ref, lb_ref, prev_ref, gn_ref, o_ref, st_ref = refs
    else:
        q_ref, a_ref, v_ref, lb_ref, o_ref, st_ref = refs
    C = chunk
    n = q_ref.shape[1] // C
    heads = q_ref.shape[2] // HEAD_DIM

    @pl.when(pl.program_id(2) == 0)
    def _():
        st_ref[...] = jnp.zeros_like(st_ref)

    row = lax.broadcasted_iota(jnp.int32, (C, HEAD_DIM), 0)
    pair_xor = lax.broadcasted_iota(jnp.int32, (C, C), 0) ^ lax.broadcasted_iota(jnp.int32, (C, C), 1)
    tt, ss = lax.broadcasted_iota(jnp.int32, (C, C), 0), lax.broadcasted_iota(jnp.int32, (C, C), 1)
    pair_level = jnp.where((tt < ss) if reverse else (tt > ss),
                           (pltpu.bitcast(pair_xor.astype(F32), jnp.int32) >> 23) - 127, -1)

    def one_chunk(j, carry):
        r0 = pl.multiple_of((n - 1 - j if reverse else j) * C, C)
        rows = pl.ds(r0, C)
        q, kk, v, near, far, tot, scores = [], [], [], [], [], [], []
        for h in range(heads):
            lanes = slice(h * HEAD_DIM, (h + 1) * HEAD_DIM)
            a, lb = a_ref[0, rows, lanes], lb_ref[:, lanes]
            e = jnp.exp(-jnp.abs(a))
            r = 1.0 / (1.0 + e)
            er = e * r
            pos = a >= 0
            f = lb + (1.0 - lb) * jnp.where(pos, r, er)
            q.append(q_ref[0, rows, lanes])
            v.append(v_ref[0, rows, lanes])
            kk.append((1.0 - lb) * jnp.where(pos, er, r))
            near.append(f)
            far.append(jnp.ones_like(f))
            tot.append(f)
            scores.append(jnp.zeros((C, C), F32))

        m, level = 1, 0
        while m < C:
            bit = (row & m) != 0
            for h in range(heads):
                qm = (q[h] * near[h]).astype(BF16)
                km = (kk[h] * far[h]).astype(BF16)
                sc = lax.dot_general(qm, km, NT_DIMS, preferred_element_type=F32)
                scores[h] = jnp.where(pair_level == level, sc, scores[h])
                tot_before = pltpu.roll(tot[h], m, 0)
                tot_after = pltpu.roll(tot[h], C - m, 0)
                if reverse:
                    near[h] = jnp.where(bit, near[h], near[h] * tot_after)
                    far[h] = jnp.where(bit, far[h] * tot_before, far[h])
                else:
                    near[h] = jnp.where(bit, near[h] * tot_before, near[h])
                    far[h] = jnp.where(bit, far[h], far[h] * tot_after)
                tot[h] = tot[h] * jnp.where(bit, tot_before, tot_after)
            m *= 2
            level += 1

        for h in range(heads):
            lanes = slice(h * HEAD_DIM, (h + 1) * HEAD_DIM)
            vb = v[h].astype(BF16)
            st = st_ref[h]
            o = jnp.dot(scores[h].astype(BF16), vb, preferred_element_type=F32)
            o = o + jnp.sum(q[h] * kk[h], axis=1, keepdims=True) * v[h]
            o = o + lax.dot_general((q[h] * near[h]).astype(BF16), st.astype(BF16), NT_DIMS,
                                    preferred_element_type=F32)
            st_ref[h] = st * tot[h][0:1] + jnp.dot(v[h].T.astype(BF16), (kk[h] * far[h]).astype(BF16),
                                                   preferred_element_type=F32)
            if finalize:
                o = o + prev_ref[0, rows, lanes]
                ms = jnp.mean(o * o, axis=-1, keepdims=True)
                o = o * lax.rsqrt(ms + EPS) * gn_ref[...]
            o_ref[0, rows, lanes] = o.astype(o_ref.dtype)
        return carry

    lax.fori_loop(0, n, one_chunk, 0)


def hgrn_direction(u, lb, gate_block, *, reverse, prev=None, o_norm=None, rows=512, chunk=128, heads=16):
    B, S, N = u.shape
    E = N // 5
    W = heads * HEAD_DIM
    HB = E // W
    R = min(rows, S)
    nb = S // R
    finalize = prev is not None
    ridx = (lambda c: nb - 1 - c) if reverse else (lambda c: c)
    col = lambda blk: pl.BlockSpec((1, R, W), lambda b, h, c: (b, ridx(c), blk * HB + h))
    in_specs = [col(0), col(gate_block), col(3), pl.BlockSpec((1, W), lambda b, h, c: (0, h))]
    args = [u, u, u, lb.reshape(1, E)]
    if finalize:
        in_specs += [pl.BlockSpec((1, R, W), lambda b, h, c: (b, ridx(c), h)),
                     pl.BlockSpec((1, HEAD_DIM), lambda b, h, c: (0, 0))]
        args += [prev, o_norm.reshape(1, HEAD_DIM)]
    return pl.pallas_call(
        functools.partial(_hgrn_kernel, reverse=reverse, finalize=finalize, chunk=min(chunk, R)),
        grid=(B, HB, nb),
        in_specs=in_specs,
        out_specs=pl.BlockSpec((1, R, W), lambda b, h, c: (b, ridx(c), h)),
        out_shape=jax.ShapeDtypeStruct((B, S, E), BF16 if finalize else F32),
        scratch_shapes=[pltpu.VMEM((heads, HEAD_DIM, HEAD_DIM), F32)],
        compiler_params=_params(("parallel", "parallel", "arbitrary"), 56),
        name="hgrn_bwd" if reverse else "hgrn_fwd",
    )(*args)


def _attn_proj_kernel(*refs, mode, scale, slab):
    if mode == "q":
        x_ref, g_ref, w_ref, t1_ref, t2_ref, gn_ref, o_ref, h_ref = refs

        @pl.when(pl.program_id(1) == 0)
        def _():
            x = x_ref[...]
            ms = jnp.mean(x * x, axis=-1, keepdims=True)
            h_ref[...] = (x * lax.rsqrt(ms + EPS) * g_ref[...]).astype(h_ref.dtype)
    elif mode == "k":
        h_ref, w_ref, t1_ref, t2_ref, gn_ref, swap_ref, o_ref = refs
    else:
        h_ref, w_ref, o_ref = refs

    half = ROPE_DIM // 2
    project = lambda c0: jnp.dot(h_ref[...], w_ref[:, c0:c0 + slab], preferred_element_type=F32)
    tn = w_ref.shape[1]
    nxt = project(0)
    for c0 in range(0, tn, slab):
        acc, nxt = nxt, (project(c0 + slab) if c0 + slab < tn else None)
        for g0 in range(0, slab, HEAD_DIM):
            y = acc[:, g0:g0 + HEAD_DIM]
            cols = slice(c0 + g0, c0 + g0 + HEAD_DIM)
            if mode == "q":
                yt = y.T
                yt = yt * lax.rsqrt(jnp.mean(yt * yt, axis=0, keepdims=True) + EPS) * (gn_ref[...] * scale)
                lo, hi, cos, sin = yt[:half], yt[half:ROPE_DIM], t1_ref[...], t2_ref[...]
                yt = jnp.concatenate([lo * cos - hi * sin, hi * cos + lo * sin, yt[ROPE_DIM:]], axis=0)
                o_ref[0, cols, :] = yt.astype(o_ref.dtype)
            elif mode == "k":
                yn = y * lax.rsqrt(jnp.mean(y * y, axis=-1, keepdims=True) + EPS) * gn_ref[...]
                partner = jnp.dot(yn.astype(BF16), swap_ref[...], preferred_element_type=F32)
                o_ref[:, cols] = (yn * t1_ref[...] + partner * t2_ref[...]).astype(o_ref.dtype)
            else:
                o_ref[0, cols, :] = y.T.astype(o_ref.dtype)


def attn_projection(x, w, mode, tables=None, head_gain=None, gain=None, *, tm=1024, tn=2048):
    B, S, D = x.shape
    E = w.shape[1]
    tm, tn = min(tm, S), min(tn, E)
    sb = S // tm
    if mode == "k":
        out_specs = [pl.BlockSpec((tm, tn), lambda i, j: (i, j))]
        out_shape = [jax.ShapeDtypeStruct((B * S, E), BF16)]
    else:
        out_specs = [pl.BlockSpec((1, tn, tm), lambda i, j: (i // sb, j, i % sb))]
        out_shape = [jax.ShapeDtypeStruct((B, E, S), BF16)]
    rows = pl.BlockSpec((tm, D), lambda i, j: (i, 0))
    weights = pl.BlockSpec((D, tn), lambda i, j: (0, j))
    half = ROPE_DIM // 2
    if mode == "q":
        in_specs = [rows, pl.BlockSpec((1, D), lambda i, j: (0, 0)), weights,
                    pl.BlockSpec((half, tm), lambda i, j: (0, i)), pl.BlockSpec((half, tm), lambda i, j: (0, i)),
                    pl.BlockSpec((HEAD_DIM, 1), lambda i, j: (0, 0))]
        args = [x.reshape(B * S, D), gain.reshape(1, D), w.astype(BF16), *tables, head_gain.reshape(HEAD_DIM, 1)]
        out_specs.append(rows)
        out_shape.append(jax.ShapeDtypeStruct((B * S, D), BF16))
    elif mode == "k":
        swap = np.zeros((HEAD_DIM, HEAD_DIM), np.float32)
        swap[np.arange(half) + half, np.arange(half)] = 1.0
        swap[np.arange(half), np.arange(half) + half] = 1.0
        tab = pl.BlockSpec((tm, LANES), lambda i, j: (i, 0))
        in_specs = [rows, weights, tab, tab, pl.BlockSpec((1, HEAD_DIM), lambda i, j: (0, 0)),
                    pl.BlockSpec((HEAD_DIM, HEAD_DIM), lambda i, j: (0, 0))]
        args = [x.reshape(B * S, D), w.astype(BF16), *tables, head_gain.reshape(1, HEAD_DIM), jnp.asarray(swap, BF16)]
    else:
        in_specs = [rows, weights]
        args = [x.reshape(B * S, D), w.astype(BF16)]
    out = pl.pallas_call(
        functools.partial(_attn_proj_kernel, mode=mode, scale=math.log2(math.e) / math.sqrt(HEAD_DIM),
                          slab=min(2 * HEAD_DIM, tn)),
        grid=(B * S // tm, E // tn),
        in_specs=in_specs,
        out_specs=out_specs,
        out_shape=out_shape,
        compiler_params=_params(("parallel", "arbitrary"), 58),
        name="attn_proj_" + mode,
    )(*args)
    if mode == "q":
        return out[0], out[1].reshape(B, S, D)
    return out[0].reshape(B, S, E) if mode == "k" else out[0]


def _matmul_kernel(h_ref, w_ref, o_ref):
    o_ref[...] = jnp.dot(h_ref[...], w_ref[...], preferred_element_type=F32)


def matmul(h2d, w, *, tm=1024, tn=2048):
    T, D = h2d.shape
    N = w.shape[1]
    tm, tn = min(tm, T), min(tn, N)
    return pl.pallas_call(
        _matmul_kernel,
        grid=(T // tm, N // tn),
        in_specs=[pl.BlockSpec((tm, D), lambda i, j: (i, 0)), pl.BlockSpec((D, tn), lambda i, j: (0, j))],
        out_specs=pl.BlockSpec((tm, tn), lambda i, j: (i, j)),
        out_shape=jax.ShapeDtypeStruct((T, N), F32),
        compiler_params=_params(("parallel", "parallel"), 48),
        name="matmul",
    )(h2d, w.astype(BF16))


def _attn_kernel(lam_ref, qt_ref, k_ref, vt_ref, sn_ref, o_ref, m_ref, l_ref, acc_ref, s00, s01, s10, s11, mc_ref,
                 *, post_scale, tq, tkc):
    tile = pl.program_id(2)
    n = k_ref.shape[1] // tkc
    s_refs = ((s00, s01), (s10, s11))
    m_ref[...] = jnp.full_like(m_ref, -jnp.inf)
    l_ref[...] = jnp.zeros_like(l_ref)
    acc_ref[...] = jnp.zeros_like(acc_ref)

    def produce(q_tile, c, slot, i):
        sl = slice(i * HEAD_DIM, (i + 1) * HEAD_DIM)
        off = pl.multiple_of(c * tkc, tkc)
        q = qt_ref[0, sl, pl.ds(pl.multiple_of(q_tile * tq, tq), tq)]
        s_new = jnp.dot(k_ref[0, pl.ds(off, tkc), sl], q, preferred_element_type=F32)
        s_refs[slot][i][...] = s_new
        mc_ref[slot, i] = jnp.max(s_new, axis=0, keepdims=True)

    def consume(vt_c, slot, i):
        m_prev = m_ref[i]
        m_new = jnp.maximum(m_prev, mc_ref[slot, i])
        alpha = jnp.exp2(m_prev - m_new)
        p = jnp.exp2(s_refs[slot][i][...] - m_new)
        l_ref[i] = alpha * l_ref[i] + jnp.sum(p, axis=0, keepdims=True)
        acc_ref[i] = alpha * acc_ref[i] + jnp.dot(vt_c, p.astype(BF16), preferred_element_type=F32)
        m_ref[i] = m_new

    def step(c, slot, next_tile, next_chunk):
        vt_c = vt_ref[0, :, pl.ds(pl.multiple_of(c * tkc, tkc), tkc)]
        for i in range(2):
            produce(next_tile, next_chunk, 1 - slot, i)
            consume(vt_c, slot, i)

    @pl.when(tile == 0)
    def _():
        produce(0, 0, 0, 0)
        produce(0, 0, 0, 1)

    def pair(j, carry):
        step(2 * j, 0, tile, 2 * j + 1)
        step(2 * j + 1, 1, tile, 2 * j + 2)
        return carry

    lax.fori_loop(0, n // 2 - 1, pair, 0)
    step(n - 2, 0, tile, n - 1)
    step(n - 1, 1, jnp.minimum(tile + 1, pl.num_programs(2) - 1), 0)
    o = (acc_ref[0] / l_ref[0] - lam_ref[0] * (acc_ref[1] / l_ref[1])).T
    ms = jnp.mean(o * o, axis=-1, keepdims=True)
    o_ref[0] = (o * lax.rsqrt(ms + EPS) * sn_ref[...] * post_scale).astype(o_ref.dtype)


def diff_attention_core(qt, kp, vt, lam, sub_norm, post_scale, *, tq=512, tkc=1024):
    B, S, E = kp.shape
    W = 2 * HEAD_DIM
    H = E // W
    tq, tkc = min(tq, S), min(tkc, S // 2)
    assert S % (2 * tkc) == 0
    return pl.pallas_call(
        functools.partial(_attn_kernel, post_scale=post_scale, tq=tq, tkc=tkc),
        grid=(B, H, S // tq),
        in_specs=[
            pl.BlockSpec(memory_space=pltpu.SMEM),
            pl.BlockSpec((1, W, S), lambda b, h, i: (b, h, 0), pipeline_mode=pl.Buffered(1)),
            pl.BlockSpec((1, S, W), lambda b, h, i: (b, 0, h)),
            pl.BlockSpec((1, W, S), lambda b, h, i: (b, h, 0)),
            pl.BlockSpec((1, W), lambda b, h, i: (0, 0)),
        ],
        out_specs=pl.BlockSpec((1, tq, W), lambda b, h, i: (b, i, h)),
        out_shape=jax.ShapeDtypeStruct((B, S, E), BF16),
        scratch_shapes=[pltpu.VMEM((2, 1, tq), F32), pltpu.VMEM((2, 1, tq), F32), pltpu.VMEM((2, W, tq), F32)]
        + [pltpu.VMEM((tkc, tq), F32)] * 4 + [pltpu.VMEM((2, 2, 1, tq), F32)],
        compiler_params=_params(("parallel", "parallel", "arbitrary"), 56),
        name="diff_attention",
    )(lam.reshape(1), qt, kp, vt, sub_norm.reshape(1, W))


def _dft_tables(n):
    ang = 2.0 * np.pi * ((np.arange(n)[:, None] * np.arange(n)[None, :]) % n) / n
    return jnp.asarray(np.cos(ang), F32), jnp.asarray(np.sin(ang), F32)


def _fnet_a_kernel(x_ref, kc_ref, ks_ref, tc_ref, ts_ref, ar_ref, ai_ref):
    n_hi, digits, ec = x_ref.shape[1], x_ref.shape[2], x_ref.shape[3]
    x = x_ref[0].reshape(n_hi * digits, ec).astype(BF16)
    ar = jnp.dot(kc_ref[...], x, preferred_element_type=F32)
    ai = -jnp.dot(ks_ref[...], x, preferred_element_type=F32)
    tc, ts = tc_ref[0], ts_ref[0]
    ar_ref[0] = (ar * tc + ai * ts).reshape(n_hi, digits, ec)
    ai_ref[0] = (ai * tc - ar * ts).reshape(n_hi, digits, ec)


def _fnet_b_kernel(ar_ref, ai_ref, wr_ref, wi_ref, cc_ref, sc_ref, perm_ref, gw_ref, y_ref, *, norm):
    wr, wi = wr_ref[...], wi_ref[...]
    digits, n_lo = ar_ref.shape[1], ar_ref.shape[2]
    G = gw_ref.shape[2]
    br, bi = [], []
    for j in range(digits):
        a = jnp.concatenate([ar_ref[0, j], ai_ref[0, j]], axis=0).astype(BF16)
        br.append(jnp.dot(wr, a, preferred_element_type=F32).astype(BF16))
        bi.append(jnp.dot(wi, a, preferred_element_type=F32).astype(BF16))
    br, bi = jnp.concatenate(br, axis=0), jnp.concatenate(bi, axis=0)
    for g in range(gw_ref.shape[0]):
        sl = slice(g * G, (g + 1) * G)
        f = (jnp.dot(br[:, sl], cc_ref[...], preferred_element_type=F32)
             + jnp.dot(bi[:, sl], sc_ref[...], preferred_element_type=F32)) * norm
        f = jnp.dot(perm_ref[...], f.astype(BF16), preferred_element_type=F32).astype(BF16)
        y_ref[0, :, :, sl] = jnp.dot(f, gw_ref[g], preferred_element_type=F32).reshape(n_lo, digits, G)


def fourier_core(u, group_w, *, n_lo=128, e_chunk=1024):
    B, S, N = u.shape
    E = N // 2
    G = E // FN_GROUPS
    n_hi = S // n_lo
    Ec = min(e_chunk, E)
    d = SUBLANES
    assert n_hi % d == 0 and n_lo % d == 0
    c_hi, s_hi = _dft_tables(n_hi)
    c_lo, s_lo = _dft_tables(n_lo)
    c_ch, s_ch = _dft_tables(G)
    eye = jnp.eye(d, dtype=F32)
    ang = 2.0 * np.pi * ((np.arange(n_lo)[:, None] * np.arange(n_hi)[None, :]) % S) / S
    tw = lambda t: jnp.asarray(t, F32).reshape(n_lo // d, d, n_hi).transpose(0, 2, 1).reshape(n_lo // d, n_hi * d, 1)
    perm = np.zeros((n_lo, d, d, n_lo), np.float32)
    perm[np.arange(n_lo)[:, None], np.arange(d)[None, :], np.arange(d)[None, :], np.arange(n_lo)[:, None]] = 1.0
    mat = lambda n: pl.BlockSpec((n, n), lambda b, j, e: (0, 0))
    tws = pl.BlockSpec((1, n_hi * d, 1), lambda b, j, e: (j, 0, 0))
    a_blk = pl.BlockSpec((1, n_hi, d, Ec), lambda b, j, e: (b, 0, j, e))
    ar, ai = pl.pallas_call(
        _fnet_a_kernel,
        grid=(B, n_lo // d, E // Ec),
        in_specs=[a_blk, mat(n_hi * d), mat(n_hi * d), tws, tws],
        out_specs=[a_blk, a_blk],
        out_shape=[jax.ShapeDtypeStruct((B, n_hi, n_lo, E), F32)] * 2,
        compiler_params=_params(("parallel", "parallel", "parallel"), 56),
        name="fnet_stage_a",
    )(u.reshape(B, n_hi, n_lo, N), jnp.kron(c_hi, eye).astype(BF16), jnp.kron(s_hi, eye).astype(BF16),
      tw(np.cos(ang)), tw(np.sin(ang)))

    a_in = pl.BlockSpec((1, d, n_lo, Ec), lambda b, j, e: (b, j, 0, e))
    wide = pl.BlockSpec((n_lo, 2 * n_lo), lambda b, j, e: (0, 0))
    y = pl.pallas_call(
        functools.partial(_fnet_b_kernel, norm=1.0 / math.sqrt(S * G)),
        grid=(B, n_hi // d, E // Ec),
        in_specs=[a_in, a_in, wide, wide, mat(G), mat(G), mat(n_lo * d),
                  pl.BlockSpec((Ec // G, G, G), lambda b, j, e: (e, 0, 0))],
        out_specs=pl.BlockSpec((1, n_lo, d, Ec), lambda b, j, e: (b, 0, j, e)),
        out_shape=jax.ShapeDtypeStruct((B, n_lo, n_hi, E), F32),
        compiler_params=_params(("parallel", "parallel", "parallel"), 56),
        name="fnet_stage_b",
    )(ar, ai, jnp.concatenate([c_lo, s_lo], axis=1).astype(BF16), jnp.concatenate([-s_lo, c_lo], axis=1).astype(BF16),
      c_ch.astype(BF16), s_ch.astype(BF16),
      jnp.asarray(perm.reshape(n_lo * d, d * n_lo), BF16), group_w.astype(BF16))
    return y.reshape(B, S, E)


def conv_layer(x, norm, w_in, dw, dw_b, ln_g, ln_b, w_out):
    B, S, D = x.shape
    E = w_out.shape[0]
    u = norm_matmul(x.reshape(B * S, D), norm, w_in)
    return conv_mixer(u.reshape(B, S, 3 * E), x.reshape(B * S, D), dw, dw_b, ln_g, ln_b, w_out).reshape(B, S, D)


def hgrn_layer(x, norm, w_in, lb_fwd, lb_bwd, o_norm, w_out):
    B, S, D = x.shape
    E = w_out.shape[0]
    u = norm_matmul(x.reshape(B * S, D), norm, w_in)
    u3 = u.reshape(B, S, 5 * E)
    o_fwd = hgrn_direction(u3, lb_fwd, 1, reverse=False)
    o = hgrn_direction(u3, lb_bwd, 2, reverse=True, prev=o_fwd, o_norm=o_norm)
    return gate_out(o.reshape(B * S, E), u, 4, x.reshape(B * S, D), w_out).reshape(B, S, D)


def rope_tables(positions):
    half = ROPE_DIM // 2
    inv = 1.0 / (ROPE_THETA ** (jnp.arange(0, ROPE_DIM, 2, dtype=F32) / ROPE_DIM))
    ang = positions.astype(F32).reshape(-1, 1) * inv
    cos, sin = jnp.cos(ang), jnp.sin(ang)
    T = ang.shape[0]
    rest = LANES - ROPE_DIM
    keep = jnp.concatenate([cos, cos, jnp.ones((T, rest), F32)], axis=-1)
    partner = jnp.concatenate([-sin, sin, jnp.zeros((T, rest), F32)], axis=-1)
    return (cos.T, sin.T), (keep, partner)


def diff_layer(x, rope, norm, w_in, q_norm, k_norm, lam_q1, lam_k1, lam_q2, lam_k2, sub_norm, w_out, lam_init):
    B, S, D = x.shape
    E = w_out.shape[0]
    wq, wk, wv, wz = (w_in[:, i * E:(i + 1) * E] for i in range(4))
    qt, h = attn_projection(x, wq, "q", rope[0], q_norm, norm)
    kp = attn_projection(h, wk, "k", rope[1], k_norm)
    vt = attn_projection(h, wv, "v")
    z = matmul(h.reshape(B * S, D), wz)
    lam = jnp.exp(jnp.sum(lam_q1 * lam_k1)) - jnp.exp(jnp.sum(lam_q2 * lam_k2)) + lam_init
    o = diff_attention_core(qt, kp, vt, lam, sub_norm, 1.0 - lam_init)
    return gate_out(o.reshape(B * S, E), z, 0, x.reshape(B * S, D), w_out).reshape(B, S, D)


def fnet_layer(x, norm, w_in, group_w, w_out):
    B, S, D = x.shape
    E = w_out.shape[0]
    u = norm_matmul(x.reshape(B * S, D), norm, w_in)
    y = fourier_core(u.reshape(B, S, 2 * E), group_w)
    return gate_out(y.reshape(B * S, E), u, 1, x.reshape(B * S, D), w_out).reshape(B, S, D)


def hgrn_lower_bounds(table):
    lb = jnp.cumsum(jax.nn.softmax(table.astype(F32), axis=0), axis=0)
    return lb - lb[0:1]


def kernel(x, positions, conv_norm, conv_w_in, conv_dw, conv_dw_b, conv_ln_g, conv_ln_b, conv_w_out, hgrn_norm, hgrn_w_in, hgrn_lb_fwd, hgrn_lb_bwd, hgrn_o_norm, hgrn_w_out, diff_norm, diff_w_in, diff_q_norm, diff_k_norm, diff_lam_q1, diff_lam_k1, diff_lam_q2, diff_lam_k2, diff_sub_norm, diff_w_out, fnet_norm, fnet_w_in, fnet_group_w, fnet_w_out):
    depth = hgrn_lb_fwd.shape[0]
    n_mixers = 4
    rope = rope_tables(positions)
    lb_fwd = hgrn_lower_bounds(hgrn_lb_fwd)
    lb_bwd = hgrn_lower_bounds(hgrn_lb_bwd)
    for layer in range(depth):
        m, j = layer % n_mixers, layer // n_mixers
        if m == 0:
            x = conv_layer(x, conv_norm[j], conv_w_in[j], conv_dw[j], conv_dw_b[j], conv_ln_g[j], conv_ln_b[j],
                           conv_w_out[j])
        elif m == 1:
            x = hgrn_layer(x, hgrn_norm[j], hgrn_w_in[j], lb_fwd[layer], lb_bwd[layer], hgrn_o_norm[j],
                           hgrn_w_out[j])
        elif m == 2:
            lam_init = 0.8 - 0.6 * math.exp(-0.3 * layer)
            x = diff_layer(x, rope, diff_norm[j], diff_w_in[j], diff_q_norm[j], diff_k_norm[j], diff_lam_q1[j],
                           diff_lam_k1[j], diff_lam_q2[j], diff_lam_k2[j], diff_sub_norm[j], diff_w_out[j], lam_init)
        else:
            x = fnet_layer(x, fnet_norm[j], fnet_w_in[j], fnet_group_w[j], fnet_w_out[j])
    return x
```
